```python
import math
import jax, jax.numpy as jnp
from jax import lax
import numpy as np

D_MODEL = 1024
BATCH = 2
SEQ = 16384
DEPTH = 2

N_MIXERS = 2
N_A_LAYERS = (DEPTH + 1) // 2
N_B_LAYERS = DEPTH // 2
HEAD_DIM = 64
N_SB_HEADS = 12
N_MLA_HEADS = 12
MLA_NOPE_DIM = 64
MLA_ROPE_DIM = 32
MLA_V_DIM = 64
MLA_Q_RANK = 384
MLA_KV_RANK = 256
ROPE_BASE = 10000.0
N_MEM = 256
N_MEM_HEADS = 4
MEM_HEAD_DIM = 64
MEM_WIDTH = N_MEM_HEADS * MEM_HEAD_DIM
SB_WIDTH = N_SB_HEADS * HEAD_DIM
MLA_WIDTH = N_MLA_HEADS * MLA_V_DIM
MIX_WIDTH = SB_WIDTH + MEM_WIDTH
SB_IN = 3 * SB_WIDTH + MEM_WIDTH
MLA_IN = MLA_Q_RANK + MLA_KV_RANK + MLA_ROPE_DIM + MEM_WIDTH
D_FF = 2816
Q_BLOCK = 128
LN_EPS = 1e-5
RMS_EPS = 1e-6
DEEPNORM_ALPHA = (2 * DEPTH) ** 0.25
DEEPNORM_BETA = (8 * DEPTH) ** -0.25

kernel_name = "hybrid_stickbreaking_mla_deepnorm_macaron"


def layer_norm(x, g, b):
    xf = x.astype(jnp.float32)
    mu = jnp.mean(xf, axis=-1, keepdims=True)
    var = jnp.mean(jnp.square(xf - mu), axis=-1, keepdims=True)
    y = (xf - mu) * lax.rsqrt(var + LN_EPS) * g.astype(jnp.float32) + b.astype(jnp.float32)
    return y.astype(x.dtype)


def rms_norm(x, g):
    xf = x.astype(jnp.float32)
    y = xf * lax.rsqrt(jnp.mean(jnp.square(xf), axis=-1, keepdims=True) + RMS_EPS)
    return (y * g.astype(jnp.float32)).astype(x.dtype)


def rope(x, pos):
    r = x.shape[-1]
    freqs = ROPE_BASE ** (-jnp.arange(0, r, 2, dtype=jnp.float32) / r)
    ang = pos.astype(jnp.float32)[:, None] * freqs[None, :]
    cos, sin = jnp.cos(ang), jnp.sin(ang)
    xf = x.astype(jnp.float32)
    x1, x2 = xf[..., : r // 2], xf[..., r // 2:]
    return jnp.concatenate([x1 * cos - x2 * sin, x1 * sin + x2 * cos], axis=-1).astype(x.dtype)


def swiglu(x, w_in, w_out):
    gate, up = jnp.split(x @ w_in, 2, axis=-1)
    return (jax.nn.silu(gate) * up) @ w_out


def heads(t, n_heads):
    b, s, hd = t.shape
    return t.reshape(b, s, n_heads, hd // n_heads).transpose(0, 2, 1, 3)


def merge_heads(o):
    b, h, s, d = o.shape
    return o.transpose(0, 2, 1, 3).reshape(b, s, h * d)


def causal_query_blocks(block_fn, n_queries):
    nb = n_queries // Q_BLOCK
    outs = [block_fn(i, i * Q_BLOCK, (i + 1) * Q_BLOCK) for i in range(nb)]
    return merge_heads(jnp.concatenate(outs, axis=2))


def stick_breaking_attention(q, k, v):
    s_len = k.shape[2]
    scale = 1.0 / math.sqrt(HEAD_DIM)
    b, h = q.shape[0], q.shape[1]
    u_in = jnp.tri(Q_BLOCK, k=-1, dtype=jnp.float32)
    hi = lax.Precision.HIGHEST

    def block(i, start, n_keys):
        nkb = i + 1
        qb = q[:, :, start:start + Q_BLOCK]
        kb, vb = k[:, :, :n_keys], v[:, :, :n_keys]
        z = jnp.einsum('bhqd,bhkd->bhqk', qb, kb).astype(jnp.float32) * scale
        mask = jnp.arange(n_keys)[None, :] < (start + jnp.arange(Q_BLOCK)[:, None])
        log_beta = jax.nn.log_sigmoid(z)
        log_keep = jnp.where(mask, log_beta - z, 0.0)
        lk = log_keep.reshape(b, h, Q_BLOCK, nkb, Q_BLOCK)
        within = jnp.einsum('bhqcj,js->bhqcs', lk, u_in, precision=hi)
        u_blk = jnp.tri(nkb, k=-1, dtype=jnp.float32)
        across = jnp.einsum('bhqc,cd->bhqd', jnp.sum(lk, axis=-1), u_blk, precision=hi)
        tail = (within + across[..., None]).reshape(b, h, Q_BLOCK, n_keys)
        a = jnp.where(mask, jnp.exp(log_beta + tail), 0.0)
        return jnp.einsum('bhqk,bhkd->bhqd', a.astype(v.dtype), vb)

    return causal_query_blocks(block, s_len)


def mla_attention(q_nope, q_rope, k_nope, k_rope, v):
    s_len = k_nope.shape[2]
    scale = 1.0 / math.sqrt(MLA_NOPE_DIM + MLA_ROPE_DIM)

    def block(i, start, n_keys):
        qn_b = q_nope[:, :, start:start + Q_BLOCK]
        qr_b = q_rope[:, :, start:start + Q_BLOCK]
        sc = (jnp.einsum('bhqd,bhkd->bhqk', qn_b, k_nope[:, :, :n_keys])
              + jnp.einsum('bhqr,bkr->bhqk', qr_b, k_rope[:, :n_keys])).astype(jnp.float32) * scale
        mask = jnp.arange(n_keys)[None, :] <= (start + jnp.arange(Q_BLOCK)[:, None])
        p = jax.nn.softmax(jnp.where(mask, sc, -jnp.inf), axis=-1)
        return jnp.einsum('bhqk,bhkd->bhqd', p.astype(v.dtype), v[:, :, :n_keys])

    return causal_query_blocks(block, s_len)


def memory_attention(q_mem, mem, w_kv):
    k_m, v_m = jnp.split(mem @ w_kv, 2, axis=-1)
    qh, kh, vh = heads(q_mem, N_MEM_HEADS), heads(k_m, N_MEM_HEADS), heads(v_m, N_MEM_HEADS)
    sc = jnp.einsum('bhqd,bhmd->bhqm', qh, kh).astype(jnp.float32) / math.sqrt(MEM_HEAD_DIM)
    p = jax.nn.softmax(sc, axis=-1)
    return merge_heads(jnp.einsum('bhqm,bhmd->bhqd', p.astype(vh.dtype), vh))


def sb_mixer(x, mem, w_in, w_kv_mem, w_out):
    h = x @ w_in
    q, k, v, q_mem = jnp.split(h, [SB_WIDTH, 2 * SB_WIDTH, 3 * SB_WIDTH], axis=-1)
    o_sb = stick_breaking_attention(heads(q, N_SB_HEADS), heads(k, N_SB_HEADS), heads(v, N_SB_HEADS))
    o_mem = memory_attention(q_mem, mem, w_kv_mem)
    return jnp.concatenate([o_sb, o_mem], axis=-1) @ w_out


def mla_mixer(x, mem, w_in, q_norm_g, w_uq, kv_norm_g, w_ukv, w_kv_mem, w_out):
    s_len = x.shape[1]
    pos = jnp.arange(s_len)
    h = x @ w_in
    c_q, c_kv, k_rope, q_mem = jnp.split(
        h, [MLA_Q_RANK, MLA_Q_RANK + MLA_KV_RANK, MLA_Q_RANK + MLA_KV_RANK + MLA_ROPE_DIM], axis=-1)
    q = heads(rms_norm(c_q, q_norm_g) @ w_uq, N_MLA_HEADS)
    q_nope, q_rope = q[..., :MLA_NOPE_DIM], rope(q[..., MLA_NOPE_DIM:], pos)
    kv = heads(rms_norm(c_kv, kv_norm_g) @ w_ukv, N_MLA_HEADS)
    k_nope, v = kv[..., :MLA_NOPE_DIM], kv[..., MLA_NOPE_DIM:]
    k_rope = rope(k_rope, pos)
    o_mla = mla_attention(q_nope, q_rope, k_nope, k_rope, v)
    o_mem = memory_attention(q_mem, mem, w_kv_mem)
    return jnp.concatenate([o_mla, o_mem], axis=-1) @ w_out


def setup_inputs(seed: int = 0) -> dict:
    key = jax.random.key(seed)
    ks = iter(jax.random.split(key, 32))

    def nrm(shape, scale):
        return jax.random.normal(next(ks), shape, jnp.float32) * scale

    def gain(shape):
        return 1.0 + nrm(shape, 0.02)

    d = D_MODEL
    return {
        "x": nrm((BATCH, SEQ, d), 1.0),
        "mem": nrm((BATCH, N_MEM, d), 1.0),
        "ln_ffn1_g": gain((DEPTH, d)),
        "ln_ffn1_b": nrm((DEPTH, d), 0.02),
        "ln_mix_g": gain((DEPTH, d)),
        "ln_mix_b": nrm((DEPTH, d), 0.02),
        "ln_ffn2_g": gain((DEPTH, d)),
        "ln_ffn2_b": nrm((DEPTH, d), 0.02),
        "ffn1_w_in": nrm((DEPTH, d, 2 * D_FF), d ** -0.5),
        "ffn1_w_out": nrm((DEPTH, D_FF, d), D_FF ** -0.5 * DEEPNORM_BETA),
        "ffn2_w_in": nrm((DEPTH, d, 2 * D_FF), d ** -0.5),
        "ffn2_w_out": nrm((DEPTH, D_FF, d), D_FF ** -0.5 * DEEPNORM_BETA),
        "sb_w_in": nrm((N_A_LAYERS, d, SB_IN), d ** -0.5),
        "mla_w_in": nrm((N_B_LAYERS, d, MLA_IN), d ** -0.5),
        "mla_q_norm_g": gain((N_B_LAYERS, MLA_Q_RANK)),
        "mla_w_uq": nrm((N_B_LAYERS, MLA_Q_RANK, N_MLA_HEADS * (MLA_NOPE_DIM + MLA_ROPE_DIM)), MLA_Q_RANK ** -0.5),
        "mla_kv_norm_g": gain((N_B_LAYERS, MLA_KV_RANK)),
        "mla_w_ukv": nrm((N_B_LAYERS, MLA_KV_RANK, N_MLA_HEADS * (MLA_NOPE_DIM + MLA_V_DIM)), MLA_KV_RANK ** -0.5),
        "mem_w_kv": nrm((DEPTH, d, 2 * MEM_WIDTH), d ** -0.5),
        "w_out": nrm((DEPTH, MIX_WIDTH, d), MIX_WIDTH ** -0.5 * DEEPNORM_BETA),
    }


def reference(x, mem, ln_ffn1_g, ln_ffn1_b, ln_mix_g, ln_mix_b, ln_ffn2_g, ln_ffn2_b,
              ffn1_w_in, ffn1_w_out, ffn2_w_in, ffn2_w_out, sb_w_in, mla_w_in,
              mla_q_norm_g, mla_w_uq, mla_kv_norm_g, mla_w_ukv, mem_w_kv, w_out):
    for i in range(DEPTH):
        x = layer_norm(DEEPNORM_ALPHA * x + 0.5 * swiglu(x, ffn1_w_in[i], ffn1_w_out[i]),
                       ln_ffn1_g[i], ln_ffn1_b[i])
        j = i // N_MIXERS
        if i % N_MIXERS == 0:
            mixed = sb_mixer(x, mem, sb_w_in[j], mem_w_kv[i], w_out[i])
        else:
            mixed = mla_mixer(x, mem, mla_w_in[j], mla_q_norm_g[j], mla_w_uq[j],
                              mla_kv_norm_g[j], mla_w_ukv[j], mem_w_kv[i], w_out[i])
        x = layer_norm(DEEPNORM_ALPHA * x + mixed, ln_mix_g[i], ln_mix_b[i])
        x = layer_norm(DEEPNORM_ALPHA * x + 0.5 * swiglu(x, ffn2_w_in[i], ffn2_w_out[i]),
                       ln_ffn2_g[i], ln_ffn2_b[i])
    return x
```

```python
import functools
import math

import jax
import jax.numpy as jnp
from jax import lax
from jax.experimental import pallas as pl
from jax.experimental.pallas import tpu as pltpu

D_MODEL = 1024
DEPTH = 2
HEAD_DIM = 64
N_SB_HEADS = 12
N_MLA_HEADS = 12
MLA_NOPE_DIM = 64
MLA_ROPE_DIM = 32
MLA_V_DIM = 64
MLA_Q_RANK = 384
MLA_KV_RANK = 256
ROPE_BASE = 10000.0
N_MEM_HEADS = 4
MEM_HEAD_DIM = 64
MEM_WIDTH = N_MEM_HEADS * MEM_HEAD_DIM
SB_WIDTH = N_SB_HEADS * HEAD_DIM
MLA_WIDTH = N_MLA_HEADS * MLA_V_DIM
D_FF = 2816
LN_EPS = 1e-5
RMS_EPS = 1e-6
DEEPNORM_ALPHA = (2 * DEPTH) ** 0.25

LANES = 128
VMEM_LIMIT_BYTES = 56 * 1024 * 1024

ROW_TILE = 512
FFN_CHUNK = D_FF // 2
SB_TILE = 256
MLA_TILE = 512
SB_LOG_WEIGHT_FLOOR = -104.0

BF16 = jnp.bfloat16
F32 = jnp.float32


def _dot(a, b):
    return jnp.dot(a, b, preferred_element_type=F32)


def _dot_nt(a, b):
    return lax.dot_general(a, b, (((1,), (1,)), ((), ())), preferred_element_type=F32)


def _layer_norm(y, g, b):
    mu = jnp.mean(y, axis=-1, keepdims=True)
    d = y - mu
    var = jnp.mean(d * d, axis=-1, keepdims=True)
    return d * lax.rsqrt(var + LN_EPS) * g + b


def _rms_norm(x, g):
    return x * lax.rsqrt(jnp.mean(x * x, axis=-1, keepdims=True) + RMS_EPS) * g


def _low_half(shape):
    return lax.broadcasted_iota(jnp.int32, shape, 1) < HEAD_DIM


def _params(*semantics):
    return pltpu.CompilerParams(dimension_semantics=semantics, vmem_limit_bytes=VMEM_LIMIT_BYTES)


def _resident(shape):
    nd = len(shape)
    return pl.BlockSpec(shape, lambda *_: (0,) * nd, pipeline_mode=pl.Buffered(1))


def _ffn_ln_kernel(x_ref, win_ref, wout_ref, g_ref, b_ref, o_ref):
    x = x_ref[...]
    xb = x.astype(BF16)
    acc = None
    for c in range(D_FF // FFN_CHUNK):
        lo, hi = c * FFN_CHUNK, (c + 1) * FFN_CHUNK
        gate = _dot(xb, win_ref[:, lo:hi])
        up = _dot(xb, win_ref[:, D_FF + lo:D_FF + hi])
        act = (gate * jax.nn.sigmoid(gate) * up).astype(BF16)
        part = _dot(act, wout_ref[lo:hi, :])
        acc = part if acc is None else acc + part
    o_ref[...] = _layer_norm(DEEPNORM_ALPHA * x + 0.5 * acc, g_ref[...], b_ref[...])


def _ffn_ln(x, w_in, w_out, g, b):
    t, d = x.shape
    tm = min(ROW_TILE, t)
    row = pl.BlockSpec((tm, d), lambda i: (i, 0))
    return pl.pallas_call(
        _ffn_ln_kernel,
        grid=(t // tm,),
        in_specs=[row, _resident(w_in.shape), _resident(w_out.shape),
                  _resident((1, d)), _resident((1, d))],
        out_specs=row,
        out_shape=jax.ShapeDtypeStruct((t, d), F32),
        compiler_params=_params("parallel"),
        name="ffn_ln",
    )(x, w_in, w_out, g.reshape(1, d), b.reshape(1, d))


def _mem_kv_kernel(mem_ref, w_ref, k_ref, v_ref):
    kv = _dot(mem_ref[...].astype(BF16), w_ref[...])
    k_ref[...] = kv[:, :MEM_WIDTH].astype(BF16)
    v_ref[...] = kv[:, MEM_WIDTH:].astype(BF16)


def _mem_kv(mem, w_kv):
    b, m, d = mem.shape
    out = pl.BlockSpec((None, m, MEM_WIDTH), lambda i: (i, 0, 0))
    return pl.pallas_call(
        _mem_kv_kernel,
        grid=(b,),
        in_specs=[pl.BlockSpec((None, m, d), lambda i: (i, 0, 0)), _resident(w_kv.shape)],
        out_specs=[out, out],
        out_shape=[jax.ShapeDtypeStruct((b, m, MEM_WIDTH), BF16)] * 2,
        compiler_params=_params("parallel"),
        name="mem_kv",
    )(mem, w_kv)


def _memory_attention(q_mem, km_ref, vm_ref):
    tm = q_mem.shape[0]
    low = _low_half((tm, LANES))
    scale = 1.0 / math.sqrt(MEM_HEAD_DIM)
    outs = []
    for p in range(MEM_WIDTH // LANES):
        q2 = q_mem[:, p * LANES:(p + 1) * LANES]
        k2 = km_ref[:, p * LANES:(p + 1) * LANES]
        v2 = vm_ref[:, p * LANES:(p + 1) * LANES]
        pair = None
        for own in (low, jnp.logical_not(low)):
            qh = jnp.where(own, q2, 0.0).astype(BF16)
            s = _dot_nt(qh, k2) * scale
            e = jnp.exp(s - jnp.max(s, axis=-1, keepdims=True))
            prob = e / jnp.sum(e, axis=-1, keepdims=True)
            o = _dot(prob.astype(BF16), v2)
            pair = o if pair is None else jnp.where(low, pair, o)
        outs.append(pair)
    return jnp.concatenate(outs, axis=-1)


def _sb_inproj_kernel(x_ref, w_ref, km_ref, vm_ref, qkv_ref, omem_ref):
    xb = x_ref[...].astype(BF16)
    q = _dot(xb, w_ref[:, :SB_WIDTH]) * (1.0 / math.sqrt(HEAD_DIM))
    qkv_ref[:, :SB_WIDTH] = q.astype(BF16)
    qkv_ref[:, SB_WIDTH:] = _dot(xb, w_ref[:, SB_WIDTH:3 * SB_WIDTH]).astype(BF16)
    q_mem = _dot(xb, w_ref[:, 3 * SB_WIDTH:])
    omem_ref[...] = _memory_attention(q_mem, km_ref, vm_ref).astype(BF16)


def _sb_inproj(x, w_in, km, vm):
    b, s, d = x.shape
    m = km.shape[1]
    tm = min(ROW_TILE, s)
    mem = pl.BlockSpec((None, m, MEM_WIDTH), lambda bi, i: (bi, 0, 0))
    return pl.pallas_call(
        _sb_inproj_kernel,
        grid=(b, s // tm),
        in_specs=[pl.BlockSpec((None, tm, d), lambda bi, i: (bi, i, 0)), _resident(w_in.shape), mem, mem],
        out_specs=[pl.BlockSpec((None, tm, 3 * SB_WIDTH), lambda bi, i: (bi, i, 0)),
                   pl.BlockSpec((None, tm, MEM_WIDTH), lambda bi, i: (bi, i, 0))],
        out_shape=[jax.ShapeDtypeStruct((b, s, 3 * SB_WIDTH), BF16),
                   jax.ShapeDtypeStruct((b, s, MEM_WIDTH), BF16)],
        compiler_params=_params("parallel", "parallel"),
        name="sb_inproj",
    )(x, w_in, km, vm)


def _split3(x):
    hi = x.astype(BF16)
    r = x - hi.astype(F32)
    mid = r.astype(BF16)
    lo = (r - mid.astype(F32)).astype(BF16)
    return hi, mid, lo


def _sb_attn_kernel(q_ref, k_ref, v_ref, o_ref, acc_ref, tail_ref):
    t = q_ref.shape[0]
    i = pl.program_id(2)
    row = lax.broadcasted_iota(jnp.int32, (t, t), 0)
    col = lax.broadcasted_iota(jnp.int32, (t, t), 1)
    later = jnp.where(row > col, 1.0, 0.0).astype(BF16)
    low = _low_half((t, LANES))
    q2 = q_ref[...].astype(F32)
    pair = None
    for own in (low, jnp.logical_not(low)):
        qh = jnp.where(own, q2, 0.0).astype(BF16)
        acc_ref[...] = jnp.zeros_like(acc_ref)
        tail_ref[...] = jnp.zeros_like(tail_ref)

        def cond(state):
            j, tail_max = state
            return jnp.logical_and(j >= 0, tail_max > SB_LOG_WEIGHT_FLOOR)

        def body(state, qh=qh):
            j, _ = state
            start = pl.multiple_of(j * t, t)
            kb = k_ref[pl.ds(start, t), :]
            vb = v_ref[pl.ds(start, t), :]
            z = _dot_nt(qh, kb)
            soft = jnp.log1p(jnp.exp(-jnp.abs(z)))
            log_beta = jnp.minimum(z, 0.0) - soft
            visible = (col - row) < (i - j) * t
            log_keep = jnp.where(visible, -(jnp.maximum(z, 0.0) + soft), 0.0)
            hi, mid, lo = _split3(log_keep)
            within = _dot(hi, later) + _dot(mid, later) + _dot(lo, later)
            tail = tail_ref[...]
            w = jnp.where(visible, jnp.exp(log_beta + within + tail), 0.0)
            acc_ref[...] += _dot(w.astype(BF16), vb)
            tail = tail + jnp.sum(log_keep, axis=-1, keepdims=True)
            tail_ref[...] = tail
            return j - 1, jnp.max(tail)

        lax.while_loop(cond, body, (i, jnp.float32(0.0)))
        pair = acc_ref[...] if pair is None else jnp.where(low, pair, acc_ref[...])
    o_ref[...] = pair.astype(o_ref.dtype)


def _sb_attention(qkv):
    b, s, _ = qkv.shape
    t = min(SB_TILE, s)
    pairs = SB_WIDTH // LANES
    return pl.pallas_call(
        _sb_attn_kernel,
        grid=(b, pairs, s // t),
        in_specs=[pl.BlockSpec((None, t, LANES), lambda bi, p, i: (bi, i, p)),
                  pl.BlockSpec((None, s, LANES), lambda bi, p, i: (bi, 0, pairs + p)),
                  pl.BlockSpec((None, s, LANES), lambda bi, p, i: (bi, 0, 2 * pairs + p))],
        out_specs=pl.BlockSpec((None, t, LANES), lambda bi, p, i: (bi, i, p)),
        out_shape=jax.ShapeDtypeStruct((b, s, SB_WIDTH), BF16),
        scratch_shapes=[pltpu.VMEM((t, LANES), F32), pltpu.VMEM((t, 1), F32)],
        compiler_params=_params("parallel", "parallel", "arbitrary"),
        name="sb_attention",
    )(qkv, qkv, qkv)


def _mla_inproj_kernel(x_ref, wcq_ref, wckv_ref, wkr_ref, wkrr_ref, wqm_ref, gq_ref, gkv_ref,
                       wq_ref, wqr_ref, wk_ref, wv_ref, cos_ref, sin_ref, km_ref, vm_ref,
                       q_ref, k_ref, v_ref, omem_ref):
    xb = x_ref[...].astype(BF16)
    cos = cos_ref[...]
    sin = sin_ref[...]
    cos2 = jnp.concatenate([cos, cos], axis=-1)
    sin2 = jnp.concatenate([sin, sin], axis=-1)
    c_q = _rms_norm(_dot(xb, wcq_ref[...]), gq_ref[...]).astype(BF16)
    c_kv = _rms_norm(_dot(xb, wckv_ref[...]), gkv_ref[...]).astype(BF16)
    k_rope = _dot(xb, wkr_ref[...]) * cos + _dot(xb, wkrr_ref[...]) * sin
    k_rope2 = jnp.concatenate([k_rope, k_rope], axis=-1)
    for p in range(N_MLA_HEADS // 2):
        lo, hi = 2 * p * LANES, 2 * (p + 1) * LANES
        q = _dot(c_q, wq_ref[:, lo:hi]) * cos2 + _dot(c_q, wqr_ref[:, lo:hi]) * sin2
        q_ref[:, lo:hi] = q.astype(BF16)
        k_ref[:, lo:hi] = (_dot(c_kv, wk_ref[:, lo:hi]) + k_rope2).astype(BF16)
    v_ref[...] = _dot(c_kv, wv_ref[...]).astype(BF16)
    q_mem = _dot(xb, wqm_ref[...])
    omem_ref[...] = _memory_attention(q_mem, km_ref, vm_ref).astype(BF16)


def _mla_inproj(x, w, cos, sin, km, vm):
    b, s, d = x.shape
    m = km.shape[1]
    tm = min(ROW_TILE, s)
    hw = N_MLA_HEADS * LANES
    mem = pl.BlockSpec((None, m, MEM_WIDTH), lambda bi, i: (bi, 0, 0))
    table = pl.BlockSpec((tm, LANES), lambda bi, i: (i, 0))

    def rows(width):
        return pl.BlockSpec((None, tm, width), lambda bi, i: (bi, i, 0))

    weights = [w["cq"], w["ckv"], w["kr"], w["kr_rot"], w["qmem"], w["gq"], w["gkv"],
               w["q"], w["q_rot"], w["k"], w["v"]]
    return pl.pallas_call(
        _mla_inproj_kernel,
        grid=(b, s // tm),
        in_specs=[rows(d)] + [_resident(a.shape) for a in weights] + [table, table, mem, mem],
        out_specs=[rows(hw), rows(hw), rows(MLA_WIDTH), rows(MEM_WIDTH)],
        out_shape=[jax.ShapeDtypeStruct((b, s, hw), BF16), jax.ShapeDtypeStruct((b, s, hw), BF16),
                   jax.ShapeDtypeStruct((b, s, MLA_WIDTH), BF16),
                   jax.ShapeDtypeStruct((b, s, MEM_WIDTH), BF16)],
        compiler_params=_params("parallel", "parallel"),
        name="mla_inproj",
    )(x, *weights, cos, sin, km, vm)


def _mla_attn_kernel(q_ref, k_ref, v_ref, o_ref, m_ref, l_ref, acc_ref):
    t = q_ref.shape[0]
    i = pl.program_id(2)
    c = math.log2(math.e) / math.sqrt(MLA_NOPE_DIM + MLA_ROPE_DIM)
    low = _low_half((t, LANES))
    row = lax.broadcasted_iota(jnp.int32, (t, t), 0)
    col = lax.broadcasted_iota(jnp.int32, (t, t), 1)
    m_ref[...] = jnp.full_like(m_ref, -jnp.inf)
    l_ref[...] = jnp.zeros_like(l_ref)
    acc_ref[...] = jnp.zeros_like(acc_ref)

    def step(j, masked):
        start = pl.multiple_of(j * t, t)
        vb = v_ref[pl.ds(start, t), :]
        for h in range(2):
            s = _dot_nt(q_ref[:, h * LANES:(h + 1) * LANES], k_ref[pl.ds(start, t), h * LANES:(h + 1) * LANES])
            if masked:
                s = jnp.where(col <= row, s, -jnp.inf)
            m_old = m_ref[h]
            m_new = jnp.maximum(m_old, jnp.max(s, axis=-1, keepdims=True))
            alpha = jnp.exp2((m_old - m_new) * c)
            prob = jnp.exp2((s - m_new) * c)
            l_ref[h] = alpha * l_ref[h] + jnp.sum(prob, axis=-1, keepdims=True)
            acc_ref[h] = alpha * acc_ref[h] + _dot(prob.astype(BF16), vb)
            m_ref[h] = m_new

    def full_step(j, carry):
        step(j, masked=False)
        return carry

    lax.fori_loop(0, i, full_step, 0)
    step(i, masked=True)
    out = jnp.where(low, acc_ref[0] / l_ref[0], acc_ref[1] / l_ref[1])
    o_ref[...] = out.astype(o_ref.dtype)


def _mla_attention(q, k, v):
    b, s, _ = q.shape
    t = min(MLA_TILE, s)
    return pl.pallas_call(
        _mla_attn_kernel,
        grid=(b, N_MLA_HEADS // 2, s // t),
        in_specs=[pl.BlockSpec((None, t, 2 * LANES), lambda bi, p, i: (bi, i, p)),
                  pl.BlockSpec((None, s, 2 * LANES), lambda bi, p, i: (bi, 0, p)),
                  pl.BlockSpec((None, s, LANES), lambda bi, p, i: (bi, 0, p))],
        out_specs=pl.BlockSpec((None, t, LANES), lambda bi, p, i: (bi, i, p)),
        out_shape=jax.ShapeDtypeStruct((b, s, MLA_WIDTH), BF16),
        scratch_shapes=[pltpu.VMEM((2, t, 1), F32), pltpu.VMEM((2, t, 1), F32),
                        pltpu.VMEM((2, t, LANES), F32)],
        compiler_params=_params("parallel", "parallel", "arbitrary"),
        name="mla_attention",
    )(q, k, v)


def _outproj_ln_kernel(x_ref, a_ref, m_ref, w_ref, g_ref, b_ref, o_ref):
    width = a_ref.shape[1]
    mixed = _dot(a_ref[...], w_ref[:width, :]) + _dot(m_ref[...], w_ref[width:, :])
    o_ref[...] = _layer_norm(DEEPNORM_ALPHA * x_ref[...] + mixed, g_ref[...], b_ref[...])


def _outproj_ln(x, attn, omem, w_out, g, b):
    t, d = x.shape
    tm = min(ROW_TILE, t)

    def rows(width):
        return pl.BlockSpec((tm, width), lambda i: (i, 0))

    return pl.pallas_call(
        _outproj_ln_kernel,
        grid=(t // tm,),
        in_specs=[rows(d), rows(attn.shape[1]), rows(omem.shape[1]), _resident(w_out.shape),
                  _resident((1, d)), _resident((1, d))],
        out_specs=rows(d),
        out_shape=jax.ShapeDtypeStruct((t, d), F32),
        compiler_params=_params("parallel"),
        name="outproj_ln",
    )(x, attn, omem, w_out, g.reshape(1, d), b.reshape(1, d))


def _rotate_half_columns(w):
    half = w.shape[1] // 2
    return jnp.concatenate([-w[:, half:], w[:, :half]], axis=1)


def _head_tiles(cols_per_head, offset):
    k, h, c = cols_per_head.shape
    tiles = jnp.zeros((k, h, LANES), cols_per_head.dtype).at[:, :, offset:offset + c].set(cols_per_head)
    return tiles.reshape(k, h * LANES)


def _mla_weights(w_in, q_norm_g, w_uq, kv_norm_g, w_ukv):
    q_dim = MLA_NOPE_DIM + MLA_ROPE_DIM
    kr0 = MLA_Q_RANK + MLA_KV_RANK
    w_kr = w_in[:, kr0:kr0 + MLA_ROPE_DIM]
    uq = w_uq.reshape(MLA_Q_RANK, N_MLA_HEADS, q_dim)
    uq_rope = uq[:, :, MLA_NOPE_DIM:]
    uq_rot = jax.vmap(_rotate_half_columns, in_axes=1, out_axes=1)(uq_rope)
    ukv = w_ukv.reshape(MLA_KV_RANK, N_MLA_HEADS, MLA_NOPE_DIM + MLA_V_DIM)
    as_tile = lambda w, off: _head_tiles(w[:, None, :], off)
    w = {
        "cq": w_in[:, :MLA_Q_RANK],
        "ckv": w_in[:, MLA_Q_RANK:kr0],
        "kr": as_tile(w_kr, MLA_NOPE_DIM),
        "kr_rot": as_tile(_rotate_half_columns(w_kr), MLA_NOPE_DIM),
        "qmem": w_in[:, kr0 + MLA_ROPE_DIM:],
        "q": _head_tiles(uq, 0),
        "q_rot": _head_tiles(uq_rot, MLA_NOPE_DIM),
        "k": _head_tiles(ukv[:, :, :MLA_NOPE_DIM], 0),
        "v": ukv[:, :, MLA_NOPE_DIM:].reshape(MLA_KV_RANK, MLA_WIDTH),
    }
    w = {name: a.astype(BF16) for name, a in w.items()}
    w["gq"] = q_norm_g.reshape(1, MLA_Q_RANK)
    w["gkv"] = kv_norm_g.reshape(1, MLA_KV_RANK)
    return w


def _rope_tables(s):
    half = MLA_ROPE_DIM // 2
    freqs = ROPE_BASE ** (-jnp.arange(0, MLA_ROPE_DIM, 2, dtype=F32) / MLA_ROPE_DIM)
    ang = jnp.arange(s).astype(F32)[:, None] * freqs[None, :]
    pad = jnp.zeros((s, LANES - MLA_NOPE_DIM - MLA_ROPE_DIM), F32)
    cos = jnp.concatenate([jnp.ones((s, MLA_NOPE_DIM), F32), jnp.cos(ang), jnp.cos(ang), pad], axis=1)
    sin = jnp.concatenate([jnp.zeros((s, MLA_NOPE_DIM), F32), jnp.sin(ang), jnp.sin(ang), pad], axis=1)
    assert half * 2 == MLA_ROPE_DIM
    return cos, sin


def kernel(x, mem, ln_ffn1_g, ln_ffn1_b, ln_mix_g, ln_mix_b, ln_ffn2_g, ln_ffn2_b, ffn1_w_in, ffn1_w_out, ffn2_w_in, ffn2_w_out, sb_w_in, mla_w_in, mla_q_norm_g, mla_w_uq, mla_kv_norm_g, mla_w_ukv, mem_w_kv, w_out):
    b, s, d = x.shape
    flat = lambda a: a.reshape(b * s, a.shape[-1])
    cos, sin = _rope_tables(s)
    for i in range(DEPTH):
        x = _ffn_ln(flat(x), ffn1_w_in[i].astype(BF16), ffn1_w_out[i].astype(BF16),
                    ln_ffn1_g[i], ln_ffn1_b[i]).reshape(b, s, d)
        km, vm = _mem_kv(mem, mem_w_kv[i].astype(BF16))
        j = i // 2
        if i % 2 == 0:
            qkv, omem = _sb_inproj(x, sb_w_in[j].astype(BF16), km, vm)
            attn = _sb_attention(qkv)
        else:
            w = _mla_weights(mla_w_in[j], mla_q_norm_g[j], mla_w_uq[j], mla_kv_norm_g[j], mla_w_ukv[j])
            q, k, v, omem = _mla_inproj(x, w, cos, sin, km, vm)
            attn = _mla_attention(q, k, v)
        x = _outproj_ln(flat(x), flat(attn), flat(omem), w_out[i].astype(BF16), ln_mix_g[i], ln_mix_b[i])
        x = _ffn_ln(x, ffn2_w_in[i].astype(BF16), ffn2_w_out[i].astype(BF16),
                    ln_ffn2_g[i], ln_ffn2_b[i]).reshape(b, s, d)
    return x
```

```python
import math

import jax
import jax.numpy as jnp
from jax import lax
from jax.experimental import pallas as pl
from jax.experimental.pallas import tpu as pltpu

D_MODEL = 1024
DEPTH = 2
HEAD_DIM = 64
N_SB_HEADS = 12
N_MLA_HEADS = 12
MLA_NOPE_DIM = 64
MLA_ROPE_DIM = 32
MLA_V_DIM = 64
MLA_Q_RANK = 384
MLA_KV_RANK = 256
ROPE_BASE = 10000.0
N_MEM_HEADS = 4
MEM_HEAD_DIM = 64
MEM_WIDTH = N_MEM_HEADS * MEM_HEAD_DIM
SB_WIDTH = N_SB_HEADS * HEAD_DIM
MLA_WIDTH = N_MLA_HEADS * MLA_V_DIM
D_FF = 2816
LN_EPS = 1e-5
RMS_EPS = 1e-6
DEEPNORM_ALPHA = (2 * DEPTH) ** 0.25

LANES = 128
VMEM_LIMIT_BYTES = 56 * 1024 * 1024

ROW_TILE = 512
FFN_CHUNK = D_FF // 2
SB_QUERY_TILE = 512
SB_SUB = LANES
SB_WINDOW = 2 * SB_SUB
MLA_TILE = 512
SB_LOG_WEIGHT_FLOOR = -88.0

BF16 = jnp.bfloat16
F32 = jnp.float32


def _dot(a, b):
    return jnp.dot(a, b, preferred_element_type=F32)


def _dot_nt(a, b):
    return lax.dot_general(a, b, (((1,), (1,)), ((), ())), preferred_element_type=F32)


def _layer_norm(y, g, b):
    mu = jnp.mean(y, axis=-1, keepdims=True)
    d = y - mu
    var = jnp.mean(d * d, axis=-1, keepdims=True)
    return d * lax.rsqrt(var + LN_EPS) * g + b


def _rms_norm(x, g):
    return x * lax.rsqrt(jnp.mean(x * x, axis=-1, keepdims=True) + RMS_EPS) * g


def _first_head(shape, axis):
    return lax.broadcasted_iota(jnp.int32, shape, axis) < HEAD_DIM


def _params(*semantics):
    return pltpu.CompilerParams(dimension_semantics=semantics, vmem_limit_bytes=VMEM_LIMIT_BYTES)


def _resident(shape):
    nd = len(shape)
    return pl.BlockSpec(shape, lambda *_: (0,) * nd, pipeline_mode=pl.Buffered(1))


def _ffn_ln_kernel(x_ref, win_ref, wout_ref, g_ref, b_ref, o_ref):
    x = x_ref[...]
    xb = x.astype(BF16)
    acc = None
    for c in range(D_FF // FFN_CHUNK):
        lo, hi = c * FFN_CHUNK, (c + 1) * FFN_CHUNK
        gate = _dot(xb, win_ref[:, lo:hi])
        up = _dot(xb, win_ref[:, D_FF + lo:D_FF + hi])
        act = (gate * jax.nn.sigmoid(gate) * up).astype(BF16)
        part = _dot(act, wout_ref[lo:hi, :])
        acc = part if acc is None else acc + part
    o_ref[...] = _layer_norm(DEEPNORM_ALPHA * x + 0.5 * acc, g_ref[...], b_ref[...])


def _ffn_ln(x, w_in, w_out, g, b):
    t, d = x.shape
    tm = min(ROW_TILE, t)
    row = pl.BlockSpec((tm, d), lambda i: (i, 0))
    return pl.pallas_call(
        _ffn_ln_kernel,
        grid=(t // tm,),
        in_specs=[row, _resident(w_in.shape), _resident(w_out.shape),
                  _resident((1, d)), _resident((1, d))],
        out_specs=row,
        out_shape=jax.ShapeDtypeStruct((t, d), F32),
        compiler_params=_params("parallel"),
        name="ffn_ln",
    )(x, w_in, w_out, g.reshape(1, d), b.reshape(1, d))


def _mem_kv_kernel(mem_ref, w_ref, k_ref, v_ref):
    kv = _dot(mem_ref[...].astype(BF16), w_ref[...])
    k_ref[...] = kv[:, :MEM_WIDTH].astype(BF16)
    v_ref[...] = kv[:, MEM_WIDTH:].astype(BF16)


def _mem_kv(mem, w_kv):
    b, m, d = mem.shape
    out = pl.BlockSpec((None, m, MEM_WIDTH), lambda i: (i, 0, 0))
    return pl.pallas_call(
        _mem_kv_kernel,
        grid=(b,),
        in_specs=[pl.BlockSpec((None, m, d), lambda i: (i, 0, 0)), _resident(w_kv.shape)],
        out_specs=[out, out],
        out_shape=[jax.ShapeDtypeStruct((b, m, MEM_WIDTH), BF16)] * 2,
        compiler_params=_params("parallel"),
        name="mem_kv",
    )(mem, w_kv)


def _memory_attention(q_mem, km_ref, vm_ref):
    tm = q_mem.shape[0]
    low = _first_head((tm, LANES), 1)
    scale = 1.0 / math.sqrt(MEM_HEAD_DIM)
    outs = []
    for p in range(MEM_WIDTH // LANES):
        q2 = q_mem[:, p * LANES:(p + 1) * LANES]
        k2 = km_ref[:, p * LANES:(p + 1) * LANES]
        v2 = vm_ref[:, p * LANES:(p + 1) * LANES]
        pair = None
        for own in (low, jnp.logical_not(low)):
            qh = jnp.where(own, q2, 0.0).astype(BF16)
            s = _dot_nt(qh, k2) * scale
            e = jnp.exp(s - jnp.max(s, axis=-1, keepdims=True))
            prob = e / jnp.sum(e, axis=-1, keepdims=True)
            o = _dot(prob.astype(BF16), v2)
            pair = o if pair is None else jnp.where(low, pair, o)
        outs.append(pair)
    return jnp.concatenate(outs, axis=-1)


def _sb_inproj_kernel(x_ref, wqt_ref, wk_ref, wvt_ref, wqm_ref, km_ref, vm_ref,
                      qt_ref, k_ref, vt_ref, omem_ref):
    xb = x_ref[...].astype(BF16)
    qt_ref[...] = (_dot_nt(wqt_ref[...], xb) * (1.0 / math.sqrt(HEAD_DIM))).astype(BF16)
    k_ref[...] = _dot(xb, wk_ref[...]).astype(BF16)
    vt_ref[...] = _dot_nt(wvt_ref[...], xb).astype(BF16)
    omem_ref[...] = _memory_attention(_dot(xb, wqm_ref[...]), km_ref, vm_ref).astype(BF16)


def _sb_inproj(x, w_in, km, vm):
    b, s, d = x.shape
    m = km.shape[1]
    tm = min(ROW_TILE, s)
    w = SB_WIDTH
    weights = [w_in[:, :w].T, w_in[:, w:2 * w], w_in[:, 2 * w:3 * w].T, w_in[:, 3 * w:]]
    mem = pl.BlockSpec((None, m, MEM_WIDTH), lambda bi, i: (bi, 0, 0))
    rows = lambda width: pl.BlockSpec((None, tm, width), lambda bi, i: (bi, i, 0))
    cols = pl.BlockSpec((None, w, tm), lambda bi, i: (bi, 0, i))
    return pl.pallas_call(
        _sb_inproj_kernel,
        grid=(b, s // tm),
        in_specs=[rows(d)] + [_resident(a.shape) for a in weights] + [mem, mem],
        out_specs=[cols, rows(w), cols, rows(MEM_WIDTH)],
        out_shape=[jax.ShapeDtypeStruct((b, w, s), BF16), jax.ShapeDtypeStruct((b, s, w), BF16),
                   jax.ShapeDtypeStruct((b, w, s), BF16), jax.ShapeDtypeStruct((b, s, MEM_WIDTH), BF16)],
        compiler_params=_params("parallel", "parallel"),
        name="sb_inproj",
    )(x, *weights, km, vm)


def _split3(x):
    hi = x.astype(BF16)
    r = x - hi.astype(F32)
    mid = r.astype(BF16)
    lo = (r - mid.astype(F32)).astype(BF16)
    return hi, mid, lo


def _later3(n):
    row = lax.broadcasted_iota(jnp.int32, (n, n), 0)
    col = lax.broadcasted_iota(jnp.int32, (n, n), 1)
    later = jnp.where(col > row, 1.0, 0.0).astype(BF16)
    return jnp.concatenate([later, later, later], axis=1)


def _sb_logs(z, visible):
    soft = jnp.log1p(jnp.exp(-jnp.abs(z)))
    return jnp.minimum(z, 0.0) - soft, jnp.where(visible, -(jnp.maximum(z, 0.0) + soft), 0.0)


def _sb_within(later3, log_keep):
    return _dot(later3, jnp.concatenate(_split3(log_keep), axis=0))


def _sb_chain(qt, k_ref, vt_ref, start, n_keys, first_query, tail_in):
    key = lax.broadcasted_iota(jnp.int32, (n_keys, SB_SUB), 0)
    qry = lax.broadcasted_iota(jnp.int32, (n_keys, SB_SUB), 1)
    z = _dot(k_ref[pl.ds(start, n_keys), :], qt)
    visible = (key - qry) < (first_query - start)
    log_beta, log_keep = _sb_logs(z, visible)
    within = _sb_within(_later3(n_keys), log_keep)
    w = jnp.where(visible, jnp.exp(log_beta + within + tail_in), 0.0)
    out_t = _dot(vt_ref[:, pl.ds(start, n_keys)], w.astype(BF16))
    return out_t, tail_in + jnp.sum(log_keep, axis=0, keepdims=True)


def _sb_attn_kernel(qt_ref, k_ref, vt_ref, o_ref, acc_ref, tail_ref):
    tq = qt_ref.shape[1]
    n_sub = tq // SB_SUB
    q0 = pl.program_id(2) * tq
    first = _first_head((LANES, SB_SUB), 0)
    owners = (first, jnp.logical_not(first))
    subs = [slice(u * SB_SUB, (u + 1) * SB_SUB) for u in range(n_sub)]

    def head_queries(u, h):
        return jnp.where(owners[h], qt_ref[:, subs[u]].astype(F32), 0.0).astype(BF16)

    starts = [pl.multiple_of(jnp.maximum(q0 + (u - 1) * SB_SUB, 0), SB_SUB) for u in range(n_sub)]
    later3 = _later3(SB_WINDOW)
    key = lax.broadcasted_iota(jnp.int32, (SB_WINDOW, 2 * SB_SUB), 0)
    qry = lax.broadcasted_iota(jnp.int32, (SB_WINDOW, 2 * SB_SUB), 1) & (SB_SUB - 1)
    ahead = key - qry
    zs = [_dot(k_ref[pl.ds(starts[u], SB_WINDOW), :],
               jnp.concatenate([head_queries(u, 0), head_queries(u, 1)], axis=1)) for u in range(n_sub)]
    stage = []
    for u in range(n_sub):
        visible = ahead < (q0 + u * SB_SUB - starts[u])
        log_beta, log_keep = _sb_logs(zs[u], visible)
        stage.append((visible, log_beta, log_keep, _sb_within(later3, log_keep)))
    pending = jnp.full((1, SB_SUB), -jnp.inf, F32)
    for u in range(n_sub):
        visible, log_beta, log_keep, within = stage[u]
        w = jnp.where(visible, jnp.exp(log_beta + within), 0.0)
        out_t = _dot(vt_ref[:, pl.ds(starts[u], SB_WINDOW)], w.astype(BF16))
        tail = jnp.sum(log_keep, axis=0, keepdims=True)
        for h in range(2):
            acc_ref[h, :, subs[u]] = out_t[:, h * SB_SUB:(h + 1) * SB_SUB]
            tail_ref[h, :, subs[u]] = tail[:, h * SB_SUB:(h + 1) * SB_SUB]
        tail_max = jnp.maximum(tail[:, :SB_SUB], tail[:, SB_SUB:])
        pending = jnp.maximum(pending, jnp.where(starts[u] > 0, tail_max, -jnp.inf))

    @pl.when(jnp.max(pending) > SB_LOG_WEIGHT_FLOOR)
    def _walk_further_back():
        for u in range(n_sub):
            for h in range(2):
                qt = head_queries(u, h)

                def cond(state):
                    start, tail_max = state
                    return jnp.logical_and(start > 0, tail_max > SB_LOG_WEIGHT_FLOOR)

                def body(state, qt=qt, u=u, h=h):
                    start = pl.multiple_of(state[0] - SB_SUB, SB_SUB)
                    out_t, tail = _sb_chain(qt, k_ref, vt_ref, start, SB_SUB,
                                            q0 + u * SB_SUB, tail_ref[h, :, subs[u]])
                    acc_ref[h, :, subs[u]] += out_t
                    tail_ref[h, :, subs[u]] = tail
                    return start, jnp.max(tail)

                lax.while_loop(cond, body, (starts[u], jnp.max(tail_ref[h, :, subs[u]])))

    out_t = jnp.where(_first_head((LANES, tq), 0), acc_ref[0], acc_ref[1])
    o_ref[...] = out_t.T.astype(o_ref.dtype)


def _sb_attention(qt, k, vt):
    b, s, _ = k.shape
    tq = min(SB_QUERY_TILE, s)
    pairs = SB_WIDTH // LANES
    return pl.pallas_call(
        _sb_attn_kernel,
        grid=(b, pairs, s // tq),
        in_specs=[pl.BlockSpec((None, LANES, tq), lambda bi, p, i: (bi, p, i)),
                  pl.BlockSpec((None, s, LANES), lambda bi, p, i: (bi, 0, p)),
                  pl.BlockSpec((None, LANES, s), lambda bi, p, i: (bi, p, 0))],
        out_specs=pl.BlockSpec((None, tq, LANES), lambda bi, p, i: (bi, i, p)),
        out_shape=jax.ShapeDtypeStruct((b, s, SB_WIDTH), BF16),
        scratch_shapes=[pltpu.VMEM((2, LANES, tq), F32), pltpu.VMEM((2, 1, tq), F32)],
        compiler_params=_params("parallel", "parallel", "arbitrary"),
        name="sb_attention",
    )(qt, k, vt)


def _mla_inproj_kernel(x_ref, wcq_ref, wckv_ref, wkr_ref, wkrr_ref, wqm_ref, gq_ref, gkv_ref,
                       wqt_ref, wqrt_ref, wk_ref, wvt_ref, cos_ref, sin_ref, cost_ref, sint_ref,
                       km_ref, vm_ref, qt_ref, k_ref, vt_ref, omem_ref):
    xb = x_ref[...].astype(BF16)
    cos, sin = cos_ref[...], sin_ref[...]
    cos_t = jnp.concatenate([cost_ref[...]] * 2, axis=0)
    sin_t = jnp.concatenate([sint_ref[...]] * 2, axis=0)
    c_q = _rms_norm(_dot(xb, wcq_ref[...]), gq_ref[...]).astype(BF16)
    c_kv = _rms_norm(_dot(xb, wckv_ref[...]), gkv_ref[...]).astype(BF16)
    k_rope = _dot(xb, wkr_ref[...]) * cos + _dot(xb, wkrr_ref[...]) * sin
    k_rope2 = jnp.concatenate([k_rope, k_rope], axis=-1)
    for p in range(N_MLA_HEADS // 2):
        lo, hi = 2 * p * LANES, 2 * (p + 1) * LANES
        q_t = _dot_nt(wqt_ref[lo:hi, :], c_q) * cos_t + _dot_nt(wqrt_ref[lo:hi, :], c_q) * sin_t
        qt_ref[lo:hi, :] = q_t.astype(BF16)
        k_ref[:, lo:hi] = (_dot(c_kv, wk_ref[:, lo:hi]) + k_rope2).astype(BF16)
    vt_ref[...] = _dot_nt(wvt_ref[...], c_kv).astype(BF16)
    omem_ref[...] = _memory_attention(_dot(xb, wqm_ref[...]), km_ref, vm_ref).astype(BF16)


def _mla_inproj(x, w, cos, sin, km, vm):
    b, s, d = x.shape
    m = km.shape[1]
    tm = min(ROW_TILE, s)
    hw = N_MLA_HEADS * LANES
    mem = pl.BlockSpec((None, m, MEM_WIDTH), lambda bi, i: (bi, 0, 0))
    table = pl.BlockSpec((tm, LANES), lambda bi, i: (i, 0))
    table_t = pl.BlockSpec((LANES, tm), lambda bi, i: (0, i))
    rows = lambda width: pl.BlockSpec((None, tm, width), lambda bi, i: (bi, i, 0))
    cols = lambda width: pl.BlockSpec((None, width, tm), lambda bi, i: (bi, 0, i))
    weights = [w["cq"], w["ckv"], w["kr"], w["kr_rot"], w["qmem"], w["gq"], w["gkv"],
               w["q"].T, w["q_rot"].T, w["k"], w["v"].T]
    return pl.pallas_call(
        _mla_inproj_kernel,
        grid=(b, s // tm),
        in_specs=([rows(d)] + [_resident(a.shape) for a in weights]
                  + [table, table, table_t, table_t, mem, mem]),
        out_specs=[cols(hw), rows(hw), cols(MLA_WIDTH), rows(MEM_WIDTH)],
        out_shape=[jax.ShapeDtypeStruct((b, hw, s), BF16), jax.ShapeDtypeStruct((b, s, hw), BF16),
                   jax.ShapeDtypeStruct((b, MLA_WIDTH, s), BF16),
                   jax.ShapeDtypeStruct((b, s, MEM_WIDTH), BF16)],
        compiler_params=_params("parallel", "parallel"),
        name="mla_inproj",
    )(x, *weights, cos, sin, cos.T, sin.T, km, vm)


def _mla_attn_kernel(qt_ref, k_ref, vt_ref, o_ref, s0_ref, s1_ref, m_ref, l_ref, acc_ref):
    t = qt_ref.shape[1]
    i = pl.program_id(2)
    c = math.log2(math.e) / math.sqrt(MLA_NOPE_DIM + MLA_ROPE_DIM)
    m_ref[...] = jnp.full_like(m_ref, -jnp.inf)
    l_ref[...] = jnp.zeros_like(l_ref)
    acc_ref[...] = jnp.zeros_like(acc_ref)

    def scores(j, s_ref):
        start = pl.multiple_of(j * t, t)
        for h in range(2):
            tile = slice(h * LANES, (h + 1) * LANES)
            s_ref[h] = _dot(k_ref[pl.ds(start, t), tile], qt_ref[tile, :])

    def update(j, s_ref, masked):
        v_t = vt_ref[:, pl.ds(pl.multiple_of(j * t, t), t)]
        for h in range(2):
            s_t = s_ref[h]
            if masked:
                key = lax.broadcasted_iota(jnp.int32, (t, t), 0)
                qry = lax.broadcasted_iota(jnp.int32, (t, t), 1)
                s_t = jnp.where(key <= qry, s_t, -jnp.inf)
            m_old = m_ref[h]
            m_new = jnp.maximum(m_old, jnp.max(s_t, axis=0, keepdims=True))
            alpha = jnp.exp2((m_old - m_new) * c)
            prob_t = jnp.exp2((s_t - m_new) * c)
            l_ref[h] = alpha * l_ref[h] + jnp.sum(prob_t, axis=0, keepdims=True)
            acc_ref[h] = alpha * acc_ref[h] + _dot(v_t, prob_t.astype(BF16))
            m_ref[h] = m_new

    def two_full_steps(jj, carry):
        j = 2 * jj
        scores(j + 1, s1_ref)
        update(j, s0_ref, masked=False)
        scores(j + 2, s0_ref)
        update(j + 1, s1_ref, masked=False)
        return carry

    scores(0, s0_ref)
    lax.fori_loop(0, i // 2, two_full_steps, 0)

    @pl.when(i % 2 == 0)
    def _diagonal_in_s0():
        update(i, s0_ref, masked=True)

    @pl.when(i % 2 == 1)
    def _last_full_then_diagonal():
        scores(i, s1_ref)
        update(i - 1, s0_ref, masked=False)
        update(i, s1_ref, masked=True)

    out_t = jnp.where(_first_head((LANES, t), 0), acc_ref[0] / l_ref[0], acc_ref[1] / l_ref[1])
    o_ref[...] = out_t.T.astype(o_ref.dtype)


def _mla_attention(qt, k, vt):
    b, s, _ = k.shape
    t = min(MLA_TILE, s)
    return pl.pallas_call(
        _mla_attn_kernel,
        grid=(b, N_MLA_HEADS // 2, s // t),
        in_specs=[pl.BlockSpec((None, 2 * LANES, t), lambda bi, p, i: (bi, p, i)),
                  pl.BlockSpec((None, s, 2 * LANES), lambda bi, p, i: (bi, 0, p)),
                  pl.BlockSpec((None, LANES, s), lambda bi, p, i: (bi, p, 0))],
        out_specs=pl.BlockSpec((None, t, LANES), lambda bi, p, i: (bi, i, p)),
        out_shape=jax.ShapeDtypeStruct((b, s, MLA_WIDTH), BF16),
        scratch_shapes=[pltpu.VMEM((2, t, t), F32), pltpu.VMEM((2, t, t), F32),
                        pltpu.VMEM((2, 1, t), F32), pltpu.VMEM((2, 1, t), F32),
                        pltpu.VMEM((2, LANES, t), F32)],
        compiler_params=_params("parallel", "parallel", "arbitrary"),
        name="mla_attention",
    )(qt, k, vt)


def _outproj_ln_kernel(x_ref, a_ref, m_ref, w_ref, g_ref, b_ref, o_ref):
    width = a_ref.shape[1]
    mixed = _dot(a_ref[...], w_ref[:width, :]) + _dot(m_ref[...], w_ref[width:, :])
    o_ref[...] = _layer_norm(DEEPNORM_ALPHA * x_ref[...] + mixed, g_ref[...], b_ref[...])


def _outproj_ln(x, attn, omem, w_out, g, b):
    t, d = x.shape
    tm = min(ROW_TILE, t)
    rows = lambda width: pl.BlockSpec((tm, width), lambda i: (i, 0))
    return pl.pallas_call(
        _outproj_ln_kernel,
        grid=(t // tm,),
        in_specs=[rows(d), rows(attn.shape[1]), rows(omem.shape[1]), _resident(w_out.shape),
                  _resident((1, d)), _resident((1, d))],
        out_specs=rows(d),
        out_shape=jax.ShapeDtypeStruct((t, d), F32),
        compiler_params=_params("parallel"),
        name="outproj_ln",
    )(x, attn, omem, w_out, g.reshape(1, d), b.reshape(1, d))


def _rotate_half_columns(w):
    half = w.shape[1] // 2
    return jnp.concatenate([-w[:, half:], w[:, :half]], axis=1)


def _head_tiles(cols_per_head, offset):
    k, h, c = cols_per_head.shape
    tiles = jnp.zeros((k, h, LANES), cols_per_head.dtype).at[:, :, offset:offset + c].set(cols_per_head)
    return tiles.reshape(k, h * LANES)


def _mla_weights(w_in, q_norm_g, w_uq, kv_norm_g, w_ukv):
    q_dim = MLA_NOPE_DIM + MLA_ROPE_DIM
    kr0 = MLA_Q_RANK + MLA_KV_RANK
    w_kr = w_in[:, kr0:kr0 + MLA_ROPE_DIM]
    uq = w_uq.reshape(MLA_Q_RANK, N_MLA_HEADS, q_dim)
    uq_rot = jax.vmap(_rotate_half_columns, in_axes=1, out_axes=1)(uq[:, :, MLA_NOPE_DIM:])
    ukv = w_ukv.reshape(MLA_KV_RANK, N_MLA_HEADS, MLA_NOPE_DIM + MLA_V_DIM)
    as_tile = lambda w, off: _head_tiles(w[:, None, :], off)
    w = {
        "cq": w_in[:, :MLA_Q_RANK],
        "ckv": w_in[:, MLA_Q_RANK:kr0],
        "kr": as_tile(w_kr, MLA_NOPE_DIM),
        "kr_rot": as_tile(_rotate_half_columns(w_kr), MLA_NOPE_DIM),
        "qmem": w_in[:, kr0 + MLA_ROPE_DIM:],
        "q": _head_tiles(uq, 0),
        "q_rot": _head_tiles(uq_rot, MLA_NOPE_DIM),
        "k": _head_tiles(ukv[:, :, :MLA_NOPE_DIM], 0),
        "v": ukv[:, :, MLA_NOPE_DIM:].reshape(MLA_KV_RANK, MLA_WIDTH),
    }
    w = {name: a.astype(BF16) for name, a in w.items()}
    w["gq"] = q_norm_g.reshape(1, MLA_Q_RANK)
    w["gkv"] = kv_norm_g.reshape(1, MLA_KV_RANK)
    return w


def _rope_tables(s):
    freqs = ROPE_BASE ** (-jnp.arange(0, MLA_ROPE_DIM, 2, dtype=F32) / MLA_ROPE_DIM)
    ang = jnp.arange(s).astype(F32)[:, None] * freqs[None, :]
    pad = jnp.zeros((s, LANES - MLA_NOPE_DIM - MLA_ROPE_DIM), F32)
    cos = jnp.concatenate([jnp.ones((s, MLA_NOPE_DIM), F32), jnp.cos(ang), jnp.cos(ang), pad], axis=1)
    sin = jnp.concatenate([jnp.zeros((s, MLA_NOPE_DIM), F32), jnp.sin(ang), jnp.sin(ang), pad], axis=1)
    return cos, sin


def kernel(x, mem, ln_ffn1_g, ln_ffn1_b, ln_mix_g, ln_mix_b, ln_ffn2_g, ln_ffn2_b, ffn1_w_in, ffn1_w_out, ffn2_w_in, ffn2_w_out, sb_w_in, mla_w_in, mla_q_norm_g, mla_w_uq, mla_kv_norm_g, mla_w_ukv, mem_w_kv, w_out):
    b, s, d = x.shape
    flat = lambda a: a.reshape(b * s, a.shape[-1])
    cos, sin = _rope_tables(s)
    for i in range(DEPTH):
        x = _ffn_ln(flat(x), ffn1_w_in[i].astype(BF16), ffn1_w_out[i].astype(BF16),
                    ln_ffn1_g[i], ln_ffn1_b[i]).reshape(b, s, d)
        km, vm = _mem_kv(mem, mem_w_kv[i].astype(BF16))
        j = i // 2
        if i % 2 == 0:
            qt, k, vt, omem = _sb_inproj(x, sb_w_in[j].astype(BF16), km, vm)
            attn = _sb_attention(qt, k, vt)
        else:
            w = _mla_weights(mla_w_in[j], mla_q_norm_g[j], mla_w_uq[j], mla_kv_norm_g[j], mla_w_ukv[j])
            qt, k, vt, omem = _mla_inproj(x, w, cos, sin, km, vm)
            attn = _mla_attention(qt, k, vt)
        x = _outproj_ln(flat(x), flat(attn), flat(omem), w_out[i].astype(BF16), ln_mix_g[i], ln_mix_b[i])
        x = _ffn_ln(x, ffn2_w_in[i].astype(BF16), ffn2_w_out[i].astype(BF16),
                    ln_ffn2_g[i], ln_ffn2_b[i]).reshape(b, s, d)
    return x
```

```python
import math

import jax
import jax.numpy as jnp
from jax import lax
from jax.experimental import pallas as pl
from jax.experimental.pallas import tpu as pltpu

D_MODEL = 1024
DEPTH = 2
HEAD_DIM = 64
N_SB_HEADS = 12
N_MLA_HEADS = 12
MLA_NOPE_DIM = 64
MLA_ROPE_DIM = 32
MLA_V_DIM = 64
MLA_Q_RANK = 384
MLA_KV_RANK = 256
ROPE_BASE = 10000.0
N_MEM_HEADS = 4
MEM_HEAD_DIM = 64
MEM_WIDTH = N_MEM_HEADS * MEM_HEAD_DIM
SB_WIDTH = N_SB_HEADS * HEAD_DIM
MLA_WIDTH = N_MLA_HEADS * MLA_V_DIM
D_FF = 2816
LN_EPS = 1e-5
RMS_EPS = 1e-6
DEEPNORM_ALPHA = (2 * DEPTH) ** 0.25

LANES = 128
VMEM_LIMIT_BYTES = 56 * 1024 * 1024

ROW_TILE = 512
FFN_CHUNK = D_FF // 2
SB_QUERY_TILE = 1024
SB_SUB = LANES
SB_WINDOW = 2 * SB_SUB
MLA_TILE = 512
SB_LOG_WEIGHT_FLOOR = -88.0
MLA_LOG2_SCALE = math.log2(math.e) / math.sqrt(MLA_NOPE_DIM + MLA_ROPE_DIM)

BF16 = jnp.bfloat16
F32 = jnp.float32


def _dot(a, b):
    return jnp.dot(a, b, preferred_element_type=F32)


def _dot_nt(a, b):
    return lax.dot_general(a, b, (((1,), (1,)), ((), ())), preferred_element_type=F32)


def _layer_norm(y, g, b):
    mu = jnp.mean(y, axis=-1, keepdims=True)
    d = y - mu
    var = jnp.mean(d * d, axis=-1, keepdims=True)
    return d * lax.rsqrt(var + LN_EPS) * g + b


def _rms_norm(x, g):
    return x * lax.rsqrt(jnp.mean(x * x, axis=-1, keepdims=True) + RMS_EPS) * g


def _first_head(shape, axis):
    return lax.broadcasted_iota(jnp.int32, shape, axis) < HEAD_DIM


def _params(*semantics):
    return pltpu.CompilerParams(dimension_semantics=semantics, vmem_limit_bytes=VMEM_LIMIT_BYTES)


def _resident(shape):
    nd = len(shape)
    return pl.BlockSpec(shape, lambda *_: (0,) * nd, pipeline_mode=pl.Buffered(1))


def _ffn_ln_kernel(x_ref, win_ref, wout_ref, g_ref, b_ref, o_ref):
    x = x_ref[...]
    xb = x.astype(BF16)
    acc = None
    for c in range(D_FF // FFN_CHUNK):
        lo, hi = c * FFN_CHUNK, (c + 1) * FFN_CHUNK
        gate = _dot(xb, win_ref[:, lo:hi])
        up = _dot(xb, win_ref[:, D_FF + lo:D_FF + hi])
        act = (gate * jax.nn.sigmoid(gate) * up).astype(BF16)
        part = _dot(act, wout_ref[lo:hi, :])
        acc = part if acc is None else acc + part
    o_ref[...] = _layer_norm(DEEPNORM_ALPHA * x + 0.5 * acc, g_ref[...], b_ref[...])


def _ffn_ln(x, w_in, w_out, g, b):
    t, d = x.shape
    tm = min(ROW_TILE, t)
    row = pl.BlockSpec((tm, d), lambda i: (i, 0))
    return pl.pallas_call(
        _ffn_ln_kernel,
        grid=(t // tm,),
        in_specs=[row, _resident(w_in.shape), _resident(w_out.shape),
                  _resident((1, d)), _resident((1, d))],
        out_specs=row,
        out_shape=jax.ShapeDtypeStruct((t, d), F32),
        compiler_params=_params("parallel"),
        name="ffn_ln",
    )(x, w_in, w_out, g.reshape(1, d), b.reshape(1, d))


def _mem_kv_kernel(mem_ref, w_ref, k_ref, v_ref):
    kv = _dot(mem_ref[...].astype(BF16), w_ref[...])
    k_ref[...] = kv[:, :MEM_WIDTH].astype(BF16)
    v_ref[...] = kv[:, MEM_WIDTH:].astype(BF16)


def _mem_kv(mem, w_kv):
    b, m, d = mem.shape
    out = pl.BlockSpec((None, m, MEM_WIDTH), lambda i: (i, 0, 0))
    return pl.pallas_call(
        _mem_kv_kernel,
        grid=(b,),
        in_specs=[pl.BlockSpec((None, m, d), lambda i: (i, 0, 0)), _resident(w_kv.shape)],
        out_specs=[out, out],
        out_shape=[jax.ShapeDtypeStruct((b, m, MEM_WIDTH), BF16)] * 2,
        compiler_params=_params("parallel"),
        name="mem_kv",
    )(mem, w_kv)


def _memory_attention(q_mem, km_ref, vm_ref):
    tm = q_mem.shape[0]
    low = _first_head((tm, LANES), 1)
    scale = 1.0 / math.sqrt(MEM_HEAD_DIM)
    outs = []
    for p in range(MEM_WIDTH // LANES):
        q2 = q_mem[:, p * LANES:(p + 1) * LANES]
        k2 = km_ref[:, p * LANES:(p + 1) * LANES]
        v2 = vm_ref[:, p * LANES:(p + 1) * LANES]
        pair = None
        for own in (low, jnp.logical_not(low)):
            qh = jnp.where(own, q2, 0.0).astype(BF16)
            s = _dot_nt(qh, k2) * scale
            e = jnp.exp(s - jnp.max(s, axis=-1, keepdims=True))
            prob = e / jnp.sum(e, axis=-1, keepdims=True)
            o = _dot(prob.astype(BF16), v2)
            pair = o if pair is None else jnp.where(low, pair, o)
        outs.append(pair)
    return jnp.concatenate(outs, axis=-1)


def _sb_inproj_kernel(x_ref, wqt_ref, wk_ref, wvt_ref, wqm_ref, km_ref, vm_ref,
                      qt_ref, k_ref, vt_ref, omem_ref):
    xb = x_ref[...].astype(BF16)
    qt_ref[...] = (_dot_nt(wqt_ref[...], xb) * (1.0 / math.sqrt(HEAD_DIM))).astype(BF16)
    k_ref[...] = _dot(xb, wk_ref[...]).astype(BF16)
    vt_ref[...] = _dot_nt(wvt_ref[...], xb).astype(BF16)
    omem_ref[...] = _memory_attention(_dot(xb, wqm_ref[...]), km_ref, vm_ref).astype(BF16)


def _sb_inproj(x, w_in, km, vm):
    b, s, d = x.shape
    m = km.shape[1]
    tm = min(ROW_TILE, s)
    w = SB_WIDTH
    weights = [w_in[:, :w].T, w_in[:, w:2 * w], w_in[:, 2 * w:3 * w].T, w_in[:, 3 * w:]]
    mem = pl.BlockSpec((None, m, MEM_WIDTH), lambda bi, i: (bi, 0, 0))
    rows = lambda width: pl.BlockSpec((None, tm, width), lambda bi, i: (bi, i, 0))
    cols = pl.BlockSpec((None, w, tm), lambda bi, i: (bi, 0, i))
    return pl.pallas_call(
        _sb_inproj_kernel,
        grid=(b, s // tm),
        in_specs=[rows(d)] + [_resident(a.shape) for a in weights] + [mem, mem],
        out_specs=[cols, rows(w), cols, rows(MEM_WIDTH)],
        out_shape=[jax.ShapeDtypeStruct((b, w, s), BF16), jax.ShapeDtypeStruct((b, s, w), BF16),
                   jax.ShapeDtypeStruct((b, w, s), BF16), jax.ShapeDtypeStruct((b, s, MEM_WIDTH), BF16)],
        compiler_params=_params("parallel", "parallel"),
        name="sb_inproj",
    )(x, *weights, km, vm)


def _split3(x):
    hi = x.astype(BF16)
    r = x - hi.astype(F32)
    mid = r.astype(BF16)
    lo = (r - mid.astype(F32)).astype(BF16)
    return hi, mid, lo


def _later3(n):
    row = lax.broadcasted_iota(jnp.int32, (n, n), 0)
    col = lax.broadcasted_iota(jnp.int32, (n, n), 1)
    later = jnp.where(col > row, 1.0, 0.0).astype(BF16)
    return jnp.concatenate([later, later, later], axis=1)


def _sb_logs(z, visible):
    soft = jnp.log(1.0 + jnp.exp(-jnp.abs(z)))
    return jnp.minimum(z, 0.0) - soft, jnp.where(visible, -(jnp.maximum(z, 0.0) + soft), 0.0)


def _sb_within(later3, log_keep):
    return _dot(later3, jnp.concatenate(_split3(log_keep), axis=0))


def _sb_chain(qt, k_ref, vt_ref, start, n_keys, first_query, tail_in):
    key = lax.broadcasted_iota(jnp.int32, (n_keys, SB_SUB), 0)
    qry = lax.broadcasted_iota(jnp.int32, (n_keys, SB_SUB), 1)
    z = _dot(k_ref[pl.ds(start, n_keys), :], qt)
    visible = (key - qry) < (first_query - start)
    log_beta, log_keep = _sb_logs(z, visible)
    within = _sb_within(_later3(n_keys), log_keep)
    w = jnp.where(visible, jnp.exp(log_beta + within + tail_in), 0.0)
    out_t = _dot(vt_ref[:, pl.ds(start, n_keys)], w.astype(BF16))
    return out_t, tail_in + jnp.sum(log_keep, axis=0, keepdims=True)


def _sb_attn_kernel(qt_ref, k_ref, vt_ref, o_ref, acc_ref, tail_ref):
    tq = qt_ref.shape[1]
    n_sub = tq // SB_SUB
    q0 = pl.program_id(2) * tq
    first = _first_head((LANES, SB_SUB), 0)
    owners = (first, jnp.logical_not(first))
    subs = [slice(u * SB_SUB, (u + 1) * SB_SUB) for u in range(n_sub)]

    def head_queries(u, h):
        return jnp.where(owners[h], qt_ref[:, subs[u]].astype(F32), 0.0).astype(BF16)

    starts = [pl.multiple_of(jnp.maximum(q0 + (u - 1) * SB_SUB, 0), SB_SUB) for u in range(n_sub)]
    later3 = _later3(SB_WINDOW)
    key = lax.broadcasted_iota(jnp.int32, (SB_WINDOW, 2 * SB_SUB), 0)
    qry = lax.broadcasted_iota(jnp.int32, (SB_WINDOW, 2 * SB_SUB), 1) & (SB_SUB - 1)
    ahead = key - qry
    zs = [_dot(k_ref[pl.ds(starts[u], SB_WINDOW), :],
               jnp.concatenate([head_queries(u, 0), head_queries(u, 1)], axis=1)) for u in range(n_sub)]
    stage = []
    for u in range(n_sub):
        visible = ahead < (q0 + u * SB_SUB - starts[u])
        log_beta, log_keep = _sb_logs(zs[u], visible)
        stage.append((visible, log_beta, log_keep, _sb_within(later3, log_keep)))
    pending = jnp.full((1, SB_SUB), -jnp.inf, F32)
    for u in range(n_sub):
        visible, log_beta, log_keep, within = stage[u]
        w = jnp.where(visible, jnp.exp(log_beta + within), 0.0)
        out_t = _dot(vt_ref[:, pl.ds(starts[u], SB_WINDOW)], w.astype(BF16))
        tail = jnp.sum(log_keep, axis=0, keepdims=True)
        for h in range(2):
            acc_ref[h, :, subs[u]] = out_t[:, h * SB_SUB:(h + 1) * SB_SUB]
            tail_ref[h, :, subs[u]] = tail[:, h * SB_SUB:(h + 1) * SB_SUB]
        tail_max = jnp.maximum(tail[:, :SB_SUB], tail[:, SB_SUB:])
        pending = jnp.maximum(pending, jnp.where(starts[u] > 0, tail_max, -jnp.inf))

    @pl.when(jnp.max(pending) > SB_LOG_WEIGHT_FLOOR)
    def _walk_further_back():
        for u in range(n_sub):
            for h in range(2):
                qt = head_queries(u, h)

                def cond(state):
                    start, tail_max = state
                    return jnp.logical_and(start > 0, tail_max > SB_LOG_WEIGHT_FLOOR)

                def body(state, qt=qt, u=u, h=h):
                    start = pl.multiple_of(state[0] - SB_SUB, SB_SUB)
                    out_t, tail = _sb_chain(qt, k_ref, vt_ref, start, SB_SUB,
                                            q0 + u * SB_SUB, tail_ref[h, :, subs[u]])
                    acc_ref[h, :, subs[u]] += out_t
                    tail_ref[h, :, subs[u]] = tail
                    return start, jnp.max(tail)

                lax.while_loop(cond, body, (starts[u], jnp.max(tail_ref[h, :, subs[u]])))

    out_t = jnp.where(_first_head((LANES, tq), 0), acc_ref[0], acc_ref[1])
    o_ref[...] = out_t.T.astype(o_ref.dtype)


def _sb_attention(qt, k, vt):
    b, s, _ = k.shape
    tq = min(SB_QUERY_TILE, s)
    pairs = SB_WIDTH // LANES
    return pl.pallas_call(
        _sb_attn_kernel,
        grid=(b, pairs, s // tq),
        in_specs=[pl.BlockSpec((None, LANES, tq), lambda bi, p, i: (bi, p, i)),
                  pl.BlockSpec((None, s, LANES), lambda bi, p, i: (bi, 0, p)),
                  pl.BlockSpec((None, LANES, s), lambda bi, p, i: (bi, p, 0))],
        out_specs=pl.BlockSpec((None, tq, LANES), lambda bi, p, i: (bi, i, p)),
        out_shape=jax.ShapeDtypeStruct((b, s, SB_WIDTH), BF16),
        scratch_shapes=[pltpu.VMEM((2, LANES, tq), F32), pltpu.VMEM((2, 1, tq), F32)],
        compiler_params=_params("parallel", "parallel", "arbitrary"),
        name="sb_attention",
    )(qt, k, vt)


def _mla_inproj_kernel(x_ref, wcq_ref, wckv_ref, wkr_ref, wkrr_ref, wqm_ref, gq_ref, gkv_ref,
                       wqt_ref, wk_ref, wvt_ref, cos_ref, sin_ref, cost_ref, sint_ref,
                       km_ref, vm_ref, qt_ref, k_ref, vt_ref, omem_ref):
    xb = x_ref[...].astype(BF16)
    cos, sin = cos_ref[...], sin_ref[...]
    cos_t, sin_t = cost_ref[...], sint_ref[...]
    c_q = _rms_norm(_dot(xb, wcq_ref[...]), gq_ref[...]).astype(BF16)
    c_kv = _rms_norm(_dot(xb, wckv_ref[...]), gkv_ref[...]).astype(BF16)
    k_rope = _dot(xb, wkr_ref[...]) * cos + _dot(xb, wkrr_ref[...]) * sin
    k_rope2 = jnp.concatenate([k_rope, k_rope], axis=-1)
    r0, r1, r2 = MLA_NOPE_DIM, MLA_NOPE_DIM + MLA_ROPE_DIM // 2, MLA_NOPE_DIM + MLA_ROPE_DIM
    for p in range(N_MLA_HEADS // 2):
        lo, hi = 2 * p * LANES, 2 * (p + 1) * LANES
        q_pair = _dot_nt(wqt_ref[lo:hi, :], c_q)
        for h in range(2):
            q_t = q_pair[h * LANES:(h + 1) * LANES]
            rotated = jnp.concatenate([q_t[:r0], -q_t[r1:r2], q_t[r0:r1], q_t[r2:]], axis=0)
            q_t = (q_t * cos_t + rotated * sin_t) * MLA_LOG2_SCALE
            qt_ref[lo + h * LANES:lo + (h + 1) * LANES, :] = q_t.astype(BF16)
        k_ref[:, lo:hi] = (_dot(c_kv, wk_ref[:, lo:hi]) + k_rope2).astype(BF16)
    v_t = _dot_nt(wvt_ref[...], c_kv).astype(BF16)
    ones = jnp.ones((LANES - MLA_V_DIM, v_t.shape[1]), BF16)
    for h in range(N_MLA_HEADS):
        vt_ref[h * LANES:h * LANES + MLA_V_DIM, :] = v_t[h * MLA_V_DIM:(h + 1) * MLA_V_DIM, :]
        vt_ref[h * LANES + MLA_V_DIM:(h + 1) * LANES, :] = ones
    omem_ref[...] = _memory_attention(_dot(xb, wqm_ref[...]), km_ref, vm_ref).astype(BF16)


def _mla_inproj(x, w, cos, sin, km, vm):
    b, s, d = x.shape
    m = km.shape[1]
    tm = min(ROW_TILE, s)
    hw = N_MLA_HEADS * LANES
    mem = pl.BlockSpec((None, m, MEM_WIDTH), lambda bi, i: (bi, 0, 0))
    table = pl.BlockSpec((tm, LANES), lambda bi, i: (i, 0))
    table_t = pl.BlockSpec((LANES, tm), lambda bi, i: (0, i))
    rows = lambda width: pl.BlockSpec((None, tm, width), lambda bi, i: (bi, i, 0))
    cols = lambda width: pl.BlockSpec((None, width, tm), lambda bi, i: (bi, 0, i))
    weights = [w["cq"], w["ckv"], w["kr"], w["kr_rot"], w["qmem"], w["gq"], w["gkv"],
               w["q"].T, w["k"], w["v"].T]
    return pl.pallas_call(
        _mla_inproj_kernel,
        grid=(b, s // tm),
        in_specs=([rows(d)] + [_resident(a.shape) for a in weights]
                  + [table, table, table_t, table_t, mem, mem]),
        out_specs=[cols(hw), rows(hw), cols(hw), rows(MEM_WIDTH)],
        out_shape=[jax.ShapeDtypeStruct((b, hw, s), BF16), jax.ShapeDtypeStruct((b, s, hw), BF16),
                   jax.ShapeDtypeStruct((b, hw, s), BF16),
                   jax.ShapeDtypeStruct((b, s, MEM_WIDTH), BF16)],
        compiler_params=_params("parallel", "parallel"),
        name="mla_inproj",
    )(x, *weights, cos, sin, cos.T, sin.T, km, vm)


def _mla_attn_kernel(qt_ref, k_ref, vt_ref, o_ref, s0_ref, s1_ref, b0_ref, b1_ref, m_ref, acc_ref):
    t = qt_ref.shape[1]
    i = pl.program_id(2)
    m_ref[...] = jnp.full_like(m_ref, -jnp.inf)
    acc_ref[...] = jnp.zeros_like(acc_ref)

    def scores(j, s_ref, b_ref):
        start = pl.multiple_of(j * t, t)
        for h in range(2):
            tile = slice(h * LANES, (h + 1) * LANES)
            s_t = _dot(k_ref[pl.ds(start, t), tile], qt_ref[tile, :])
            s_ref[h] = s_t
            b_ref[h] = jnp.max(s_t, axis=0, keepdims=True)

    def update(j, s_ref, b_ref, masked):
        start = pl.multiple_of(j * t, t)
        for h in range(2):
            s_t = s_ref[h]
            if masked:
                key = lax.broadcasted_iota(jnp.int32, (t, t), 0)
                qry = lax.broadcasted_iota(jnp.int32, (t, t), 1)
                s_t = jnp.where(key <= qry, s_t, -jnp.inf)
                block_max = jnp.max(s_t, axis=0, keepdims=True)
            else:
                block_max = b_ref[h]
            m_old = m_ref[h]
            m_new = jnp.maximum(m_old, block_max)
            alpha = jnp.exp2(m_old - m_new)
            prob_t = jnp.exp2(s_t - m_new)
            v_t = vt_ref[h * LANES:(h + 1) * LANES, pl.ds(start, t)]
            acc_ref[h] = alpha * acc_ref[h] + _dot(v_t, prob_t.astype(BF16))
            m_ref[h] = m_new

    def two_full_steps(j):
        scores(j + 1, s1_ref, b1_ref)
        update(j, s0_ref, b0_ref, masked=False)
        scores(j + 2, s0_ref, b0_ref)
        update(j + 1, s1_ref, b1_ref, masked=False)

    def four_full_steps(jj, carry):
        two_full_steps(4 * jj)
        two_full_steps(4 * jj + 2)
        return carry

    scores(0, s0_ref, b0_ref)
    lax.fori_loop(0, i // 4, four_full_steps, 0)

    @pl.when(i % 4 >= 2)
    def _two_more_full_steps():
        two_full_steps(4 * (i // 4))

    @pl.when(i % 2 == 0)
    def _diagonal_in_s0():
        update(i, s0_ref, b0_ref, masked=True)

    @pl.when(i % 2 == 1)
    def _last_full_then_diagonal():
        scores(i, s1_ref, b1_ref)
        update(i - 1, s0_ref, b0_ref, masked=False)
        update(i, s1_ref, b1_ref, masked=True)

    heads = [acc_ref[h, :MLA_V_DIM, :] / acc_ref[h, MLA_V_DIM:MLA_V_DIM + 1, :] for h in range(2)]
    o_ref[...] = jnp.concatenate(heads, axis=0).T.astype(o_ref.dtype)


def _mla_attention(qt, k, vt):
    b, s, _ = k.shape
    t = min(MLA_TILE, s)
    return pl.pallas_call(
        _mla_attn_kernel,
        grid=(b, N_MLA_HEADS // 2, s // t),
        in_specs=[pl.BlockSpec((None, 2 * LANES, t), lambda bi, p, i: (bi, p, i)),
                  pl.BlockSpec((None, s, 2 * LANES), lambda bi, p, i: (bi, 0, p)),
                  pl.BlockSpec((None, 2 * LANES, s), lambda bi, p, i: (bi, p, 0))],
        out_specs=pl.BlockSpec((None, t, LANES), lambda bi, p, i: (bi, i, p)),
        out_shape=jax.ShapeDtypeStruct((b, s, MLA_WIDTH), BF16),
        scratch_shapes=[pltpu.VMEM((2, t, t), F32), pltpu.VMEM((2, t, t), F32),
                        pltpu.VMEM((2, 1, t), F32), pltpu.VMEM((2, 1, t), F32),
                        pltpu.VMEM((2, 1, t), F32), pltpu.VMEM((2, LANES, t), F32)],
        compiler_params=_params("parallel", "parallel", "arbitrary"),
        name="mla_attention",
    )(qt, k, vt)


def _outproj_ln_kernel(x_ref, a_ref, m_ref, w_ref, g_ref, b_ref, o_ref):
    width = a_ref.shape[1]
    mixed = _dot(a_ref[...], w_ref[:width, :]) + _dot(m_ref[...], w_ref[width:, :])
    o_ref[...] = _layer_norm(DEEPNORM_ALPHA * x_ref[...] + mixed, g_ref[...], b_ref[...])


def _outproj_ln(x, attn, omem, w_out, g, b):
    t, d = x.shape
    tm = min(ROW_TILE, t)
    rows = lambda width: pl.BlockSpec((tm, width), lambda i: (i, 0))
    return pl.pallas_call(
        _outproj_ln_kernel,
        grid=(t // tm,),
        in_specs=[rows(d), rows(attn.shape[1]), rows(omem.shape[1]), _resident(w_out.shape),
                  _resident((1, d)), _resident((1, d))],
        out_specs=rows(d),
        out_shape=jax.ShapeDtypeStruct((t, d), F32),
        compiler_params=_params("parallel"),
        name="outproj_ln",
    )(x, attn, omem, w_out, g.reshape(1, d), b.reshape(1, d))


def _rotate_half_columns(w):
    half = w.shape[1] // 2
    return jnp.concatenate([-w[:, half:], w[:, :half]], axis=1)


def _head_tiles(cols_per_head, offset):
    k, h, c = cols_per_head.shape
    tiles = jnp.zeros((k, h, LANES), cols_per_head.dtype).at[:, :, offset:offset + c].set(cols_per_head)
    return tiles.reshape(k, h * LANES)


def _mla_weights(w_in, q_norm_g, w_uq, kv_norm_g, w_ukv):
    q_dim = MLA_NOPE_DIM + MLA_ROPE_DIM
    kr0 = MLA_Q_RANK + MLA_KV_RANK
    w_kr = w_in[:, kr0:kr0 + MLA_ROPE_DIM]
    uq = w_uq.reshape(MLA_Q_RANK, N_MLA_HEADS, q_dim)
    ukv = w_ukv.reshape(MLA_KV_RANK, N_MLA_HEADS, MLA_NOPE_DIM + MLA_V_DIM)
    as_tile = lambda w, off: _head_tiles(w[:, None, :], off)
    w = {
        "cq": w_in[:, :MLA_Q_RANK],
        "ckv": w_in[:, MLA_Q_RANK:kr0],
        "kr": as_tile(w_kr, MLA_NOPE_DIM),
        "kr_rot": as_tile(_rotate_half_columns(w_kr), MLA_NOPE_DIM),
        "qmem": w_in[:, kr0 + MLA_ROPE_DIM:],
        "q": _head_tiles(uq, 0),
        "k": _head_tiles(ukv[:, :, :MLA_NOPE_DIM], 0),
        "v": ukv[:, :, MLA_NOPE_DIM:].reshape(MLA_KV_RANK, MLA_WIDTH),
    }
    w = {name: a.astype(BF16) for name, a in w.items()}
    w["gq"] = q_norm_g.reshape(1, MLA_Q_RANK)
    w["gkv"] = kv_norm_g.reshape(1, MLA_KV_RANK)
    return w


def _rope_tables(s):
    freqs = ROPE_BASE ** (-jnp.arange(0, MLA_ROPE_DIM, 2, dtype=F32) / MLA_ROPE_DIM)
    ang = jnp.arange(s).astype(F32)[:, None] * freqs[None, :]
    pad = jnp.zeros((s, LANES - MLA_NOPE_DIM - MLA_ROPE_DIM), F32)
    cos = jnp.concatenate([jnp.ones((s, MLA_NOPE_DIM), F32), jnp.cos(ang), jnp.cos(ang), pad], axis=1)
    sin = jnp.concatenate([jnp.zeros((s, MLA_NOPE_DIM), F32), jnp.sin(ang), jnp.sin(ang), pad], axis=1)
    return cos, sin


def kernel(x, mem, ln_ffn1_g, ln_ffn1_b, ln_mix_g, ln_mix_b, ln_ffn2_g, ln_ffn2_b, ffn1_w_in, ffn1_w_out, ffn2_w_in, ffn2_w_out, sb_w_in, mla_w_in, mla_q_norm_g, mla_w_uq, mla_kv_norm_g, mla_w_ukv, mem_w_kv, w_out):
    b, s, d = x.shape
    flat = lambda a: a.reshape(b * s, a.shape[-1])
    cos, sin = _rope_tables(s)
    for i in range(DEPTH):
        x = _ffn_ln(flat(x), ffn1_w_in[i].astype(BF16), ffn1_w_out[i].astype(BF16),
                    ln_ffn1_g[i], ln_ffn1_b[i]).reshape(b, s, d)
        km, vm = _mem_kv(mem, mem_w_kv[i].astype(BF16))
        j = i // 2
        if i % 2 == 0:
            qt, k, vt, omem = _sb_inproj(x, sb_w_in[j].astype(BF16), km, vm)
            attn = _sb_attention(qt, k, vt)
        else:
            w = _mla_weights(mla_w_in[j], mla_q_norm_g[j], mla_w_uq[j], mla_kv_norm_g[j], mla_w_ukv[j])
            qt, k, vt, omem = _mla_inproj(x, w, cos, sin, km, vm)
            attn = _mla_attention(qt, k, vt)
        x = _outproj_ln(flat(x), flat(attn), flat(omem), w_out[i].astype(BF16), ln_mix_g[i], ln_mix_b[i])
        x = _ffn_ln(x, ffn2_w_in[i].astype(BF16), ffn2_w_out[i].astype(BF16),
                    ln_ffn2_g[i], ln_ffn2_b[i]).reshape(b, s, d)
    return x
```

```python
import math

import jax
import jax.numpy as jnp
from jax import lax
from jax.experimental import pallas as pl
from jax.experimental.pallas import tpu as pltpu

D_MODEL = 1024
DEPTH = 2
HEAD_DIM = 64
N_SB_HEADS = 12
N_MLA_HEADS = 12
MLA_NOPE_DIM = 64
MLA_ROPE_DIM = 32
MLA_V_DIM = 64
MLA_Q_RANK = 384
MLA_KV_RANK = 256
ROPE_BASE = 10000.0
N_MEM_HEADS = 4
MEM_HEAD_DIM = 64
MEM_WIDTH = N_MEM_HEADS * MEM_HEAD_DIM
SB_WIDTH = N_SB_HEADS * HEAD_DIM
MLA_WIDTH = N_MLA_HEADS * MLA_V_DIM
D_FF = 2816
LN_EPS = 1e-5
RMS_EPS = 1e-6
DEEPNORM_ALPHA = (2 * DEPTH) ** 0.25

LANES = 128
VMEM_LIMIT_BYTES = 56 * 1024 * 1024

ROW_TILE = 512
MXU_TILE = 256
FFN_SPLIT = (0, 6 * MXU_TILE, D_FF)
BF16_ROWS = 16
MLA_VT_ROWS = MLA_V_DIM + BF16_ROWS
SB_QUERY_TILE = 1024
SB_SUB = LANES
SB_WINDOW = 2 * SB_SUB
MLA_TILE = 512
SB_LOG_WEIGHT_FLOOR = -88.0
MLA_LOG2_SCALE = math.log2(math.e) / math.sqrt(MLA_NOPE_DIM + MLA_ROPE_DIM)

BF16 = jnp.bfloat16
F32 = jnp.float32


def _dot(a, b):
    return jnp.dot(a, b, preferred_element_type=F32)


def _dot_nt(a, b):
    return lax.dot_general(a, b, (((1,), (1,)), ((), ())), preferred_element_type=F32)


def _layer_norm(y, g, b):
    mu = jnp.mean(y, axis=-1, keepdims=True)
    d = y - mu
    var = jnp.mean(d * d, axis=-1, keepdims=True)
    return d * lax.rsqrt(var + LN_EPS) * g + b


def _rms_norm(x, g):
    return x * lax.rsqrt(jnp.mean(x * x, axis=-1, keepdims=True) + RMS_EPS) * g


def _first_head(shape, axis):
    return lax.broadcasted_iota(jnp.int32, shape, axis) < HEAD_DIM


def _params(*semantics):
    return pltpu.CompilerParams(dimension_semantics=semantics, vmem_limit_bytes=VMEM_LIMIT_BYTES)


def _resident(shape):
    nd = len(shape)
    return pl.BlockSpec(shape, lambda *_: (0,) * nd, pipeline_mode=pl.Buffered(1))


def _ffn_block(x, win_ref, wout_ref, g_ref, b_ref):
    xb = x.astype(BF16)
    acc = None
    for lo, hi in zip(FFN_SPLIT[:-1], FFN_SPLIT[1:]):
        gate = _dot(xb, win_ref[:, lo:hi])
        up = _dot(xb, win_ref[:, D_FF + lo:D_FF + hi])
        act = (gate * jax.nn.sigmoid(gate) * up).astype(BF16)
        part = _dot(act, wout_ref[lo:hi, :])
        acc = part if acc is None else acc + part
    return _layer_norm(DEEPNORM_ALPHA * x + 0.5 * acc, g_ref[...], b_ref[...])


def _ffn_ln_kernel(x_ref, win_ref, wout_ref, g_ref, b_ref, o_ref):
    o_ref[...] = _ffn_block(x_ref[...], win_ref, wout_ref, g_ref, b_ref)


def _ffn_ln(x, w_in, w_out, g, b):
    t, d = x.shape
    tm = min(ROW_TILE, t)
    row = pl.BlockSpec((tm, d), lambda i: (i, 0))
    return pl.pallas_call(
        _ffn_ln_kernel,
        grid=(t // tm,),
        in_specs=[row, _resident(w_in.shape), _resident(w_out.shape),
                  _resident((1, d)), _resident((1, d))],
        out_specs=row,
        out_shape=jax.ShapeDtypeStruct((t, d), F32),
        compiler_params=_params("parallel"),
        name="ffn_ln",
    )(x, w_in, w_out, g.reshape(1, d), b.reshape(1, d))


def _mem_kv_kernel(mem_ref, w_ref, k_ref, v_ref):
    kv = _dot(mem_ref[...].astype(BF16), w_ref[...])
    k_ref[...] = kv[:, :MEM_WIDTH].astype(BF16)
    v_ref[...] = kv[:, MEM_WIDTH:].astype(BF16)


def _mem_kv(mem, w_kv):
    b, m, d = mem.shape
    out = pl.BlockSpec((None, m, MEM_WIDTH), lambda i: (i, 0, 0))
    return pl.pallas_call(
        _mem_kv_kernel,
        grid=(b,),
        in_specs=[pl.BlockSpec((None, m, d), lambda i: (i, 0, 0)), _resident(w_kv.shape)],
        out_specs=[out, out],
        out_shape=[jax.ShapeDtypeStruct((b, m, MEM_WIDTH), BF16)] * 2,
        compiler_params=_params("parallel"),
        name="mem_kv",
    )(mem, w_kv)


def _memory_attention(q_mem, km_ref, vm_ref):
    tm = q_mem.shape[0]
    low = _first_head((tm, LANES), 1)
    scale = 1.0 / math.sqrt(MEM_HEAD_DIM)
    outs = []
    for p in range(MEM_WIDTH // LANES):
        q2 = q_mem[:, p * LANES:(p + 1) * LANES]
        k2 = km_ref[:, p * LANES:(p + 1) * LANES]
        v2 = vm_ref[:, p * LANES:(p + 1) * LANES]
        pair = None
        for own in (low, jnp.logical_not(low)):
            qh = jnp.where(own, q2, 0.0).astype(BF16)
            s = _dot_nt(qh, k2) * scale
            e = jnp.exp(s - jnp.max(s, axis=-1, keepdims=True))
            prob = e / jnp.sum(e, axis=-1, keepdims=True)
            o = _dot(prob.astype(BF16), v2)
            pair = o if pair is None else jnp.where(low, pair, o)
        outs.append(pair)
    return jnp.concatenate(outs, axis=-1)


def _sb_inproj_kernel(x_ref, wqt_ref, wk_ref, wvt_ref, wqm_ref, km_ref, vm_ref,
                      qt_ref, k_ref, vt_ref, omem_ref):
    xb = x_ref[...].astype(BF16)
    qt_ref[...] = (_dot_nt(wqt_ref[...], xb) * (1.0 / math.sqrt(HEAD_DIM))).astype(BF16)
    k_ref[...] = _dot(xb, wk_ref[...]).astype(BF16)
    vt_ref[...] = _dot_nt(wvt_ref[...], xb).astype(BF16)
    omem_ref[...] = _memory_attention(_dot(xb, wqm_ref[...]), km_ref, vm_ref).astype(BF16)


def _sb_inproj(x, w_in, km, vm):
    b, s, d = x.shape
    m = km.shape[1]
    tm = min(ROW_TILE, s)
    w = SB_WIDTH
    weights = [w_in[:, :w].T, w_in[:, w:2 * w], w_in[:, 2 * w:3 * w].T, w_in[:, 3 * w:]]
    mem = pl.BlockSpec((None, m, MEM_WIDTH), lambda bi, i: (bi, 0, 0))
    rows = lambda width: pl.BlockSpec((None, tm, width), lambda bi, i: (bi, i, 0))
    cols = pl.BlockSpec((None, w, tm), lambda bi, i: (bi, 0, i))
    return pl.pallas_call(
        _sb_inproj_kernel,
        grid=(b, s // tm),
        in_specs=[rows(d)] + [_resident(a.shape) for a in weights] + [mem, mem],
        out_specs=[cols, rows(w), cols, rows(MEM_WIDTH)],
        out_shape=[jax.ShapeDtypeStruct((b, w, s), BF16), jax.ShapeDtypeStruct((b, s, w), BF16),
                   jax.ShapeDtypeStruct((b, w, s), BF16), jax.ShapeDtypeStruct((b, s, MEM_WIDTH), BF16)],
        compiler_params=_params("parallel", "parallel"),
        name="sb_inproj",
    )(x, *weights, km, vm)


def _split3(x):
    hi = x.astype(BF16)
    r = x - hi.astype(F32)
    mid = r.astype(BF16)
    lo = (r - mid.astype(F32)).astype(BF16)
    return hi, mid, lo


def _later3(n):
    row = lax.broadcasted_iota(jnp.int32, (n, n), 0)
    col = lax.broadcasted_iota(jnp.int32, (n, n), 1)
    later = jnp.where(col > row, 1.0, 0.0).astype(BF16)
    return jnp.concatenate([later, later, later], axis=1)


def _sb_logs(z, visible):
    decay = jnp.log(1.0 + jnp.exp2(jnp.abs(z) * -math.log2(math.e))) + jnp.maximum(z, 0.0)
    return z - decay, jnp.where(visible, decay, 0.0)


def _sb_behind(later3, decay):
    return _dot(later3, jnp.concatenate(_split3(decay), axis=0))


def _sb_chain(qt, k_ref, vt_ref, start, n_keys, first_query, tail_in):
    key = lax.broadcasted_iota(jnp.int32, (n_keys, SB_SUB), 0)
    qry = lax.broadcasted_iota(jnp.int32, (n_keys, SB_SUB), 1)
    z = _dot(k_ref[pl.ds(start, n_keys), :], qt)
    visible = (key - qry) < (first_query - start)
    log_beta, decay = _sb_logs(z, visible)
    behind = _sb_behind(_later3(n_keys), decay)
    w = jnp.where(visible, jnp.exp(log_beta - behind + tail_in), 0.0)
    out_t = _dot(vt_ref[:, pl.ds(start, n_keys)], w.astype(BF16))
    return out_t, tail_in - jnp.sum(decay, axis=0, keepdims=True)


def _sb_attn_kernel(qt_ref, k_ref, vt_ref, o_ref, acc_ref, tail_ref):
    tq = qt_ref.shape[1]
    n_sub = tq // SB_SUB
    q0 = pl.program_id(2) * tq
    first = _first_head((LANES, SB_SUB), 0)
    owners = (first, jnp.logical_not(first))
    subs = [slice(u * SB_SUB, (u + 1) * SB_SUB) for u in range(n_sub)]

    def head_queries(u, h):
        return jnp.where(owners[h], qt_ref[:, subs[u]].astype(F32), 0.0).astype(BF16)

    starts = [pl.multiple_of(jnp.maximum(q0 + (u - 1) * SB_SUB, 0), SB_SUB) for u in range(n_sub)]
    later3 = _later3(SB_WINDOW)
    key = lax.broadcasted_iota(jnp.int32, (SB_WINDOW, 2 * SB_SUB), 0)
    qry = lax.broadcasted_iota(jnp.int32, (SB_WINDOW, 2 * SB_SUB), 1) & (SB_SUB - 1)
    ahead = key - qry
    zs = [_dot(k_ref[pl.ds(starts[u], SB_WINDOW), :],
               jnp.concatenate([head_queries(u, 0), head_queries(u, 1)], axis=1)) for u in range(n_sub)]
    stage = []
    for u in range(n_sub):
        visible = ahead < (q0 + u * SB_SUB - starts[u])
        log_beta, decay = _sb_logs(zs[u], visible)
        stage.append((visible, log_beta, decay, _sb_behind(later3, decay)))
    pending = jnp.full((1, SB_SUB), -jnp.inf, F32)
    for u in range(n_sub):
        visible, log_beta, decay, behind = stage[u]
        w = jnp.where(visible, jnp.exp(log_beta - behind), 0.0)
        out_t = _dot(vt_ref[:, pl.ds(starts[u], SB_WINDOW)], w.astype(BF16))
        tail = -jnp.sum(decay, axis=0, keepdims=True)
        for h in range(2):
            acc_ref[h, :, subs[u]] = out_t[:, h * SB_SUB:(h + 1) * SB_SUB]
            tail_ref[h, :, subs[u]] = tail[:, h * SB_SUB:(h + 1) * SB_SUB]
        tail_max = jnp.maximum(tail[:, :SB_SUB], tail[:, SB_SUB:])
        pending = jnp.maximum(pending, jnp.where(starts[u] > 0, tail_max, -jnp.inf))

    @pl.when(jnp.max(pending) > SB_LOG_WEIGHT_FLOOR)
    def _walk_further_back():
        for u in range(n_sub):
            for h in range(2):
                qt = head_queries(u, h)

                def cond(state):
                    start, tail_max = state
                    return jnp.logical_and(start > 0, tail_max > SB_LOG_WEIGHT_FLOOR)

                def body(state, qt=qt, u=u, h=h):
                    start = pl.multiple_of(state[0] - SB_SUB, SB_SUB)
                    out_t, tail = _sb_chain(qt, k_ref, vt_ref, start, SB_SUB,
                                            q0 + u * SB_SUB, tail_ref[h, :, subs[u]])
                    acc_ref[h, :, subs[u]] += out_t
                    tail_ref[h, :, subs[u]] = tail
                    return start, jnp.max(tail)

                lax.while_loop(cond, body, (starts[u], jnp.max(tail_ref[h, :, subs[u]])))

    out_t = jnp.where(_first_head((LANES, tq), 0), acc_ref[0], acc_ref[1])
    o_ref[...] = out_t.T.astype(o_ref.dtype)


def _sb_attention(qt, k, vt):
    b, s, _ = k.shape
    tq = min(SB_QUERY_TILE, s)
    pairs = SB_WIDTH // LANES
    return pl.pallas_call(
        _sb_attn_kernel,
        grid=(b, pairs, s // tq),
        in_specs=[pl.BlockSpec((None, LANES, tq), lambda bi, p, i: (bi, p, i)),
                  pl.BlockSpec((None, s, LANES), lambda bi, p, i: (bi, 0, p)),
                  pl.BlockSpec((None, LANES, s), lambda bi, p, i: (bi, p, 0))],
        out_specs=pl.BlockSpec((None, tq, LANES), lambda bi, p, i: (bi, i, p)),
        out_shape=jax.ShapeDtypeStruct((b, s, SB_WIDTH), BF16),
        scratch_shapes=[pltpu.VMEM((2, LANES, tq), F32), pltpu.VMEM((2, 1, tq), F32)],
        compiler_params=_params("parallel", "parallel", "arbitrary"),
        name="sb_attention",
    )(qt, k, vt)


def _mla_inproj_kernel(x_ref, wcq_ref, wckv_ref, wkr_ref, wkrr_ref, wqm_ref, gq_ref, gkv_ref,
                       wqt_ref, wk_ref, wvt_ref, cos_ref, sin_ref, cost_ref, sint_ref,
                       km_ref, vm_ref, qt_ref, k_ref, vt_ref, omem_ref):
    xb = x_ref[...].astype(BF16)
    cos, sin = cos_ref[...], sin_ref[...]
    cos_t, sin_t = cost_ref[...], sint_ref[...]
    c_q = _rms_norm(_dot(xb, wcq_ref[...]), gq_ref[...]).astype(BF16)
    c_kv = _rms_norm(_dot(xb, wckv_ref[...]), gkv_ref[...]).astype(BF16)
    k_rope = _dot(xb, wkr_ref[...]) * cos + _dot(xb, wkrr_ref[...]) * sin
    k_rope2 = jnp.concatenate([k_rope, k_rope], axis=-1)
    r0, r1, r2 = MLA_NOPE_DIM, MLA_NOPE_DIM + MLA_ROPE_DIM // 2, MLA_NOPE_DIM + MLA_ROPE_DIM
    for p in range(N_MLA_HEADS // 2):
        lo, hi = 2 * p * LANES, 2 * (p + 1) * LANES
        q_pair = _dot_nt(wqt_ref[lo:hi, :], c_q)
        for h in range(2):
            q_t = q_pair[h * LANES:(h + 1) * LANES]
            rotated = jnp.concatenate([q_t[:r0], -q_t[r1:r2], q_t[r0:r1], q_t[r2:]], axis=0)
            q_t = (q_t * cos_t + rotated * sin_t) * MLA_LOG2_SCALE
            qt_ref[lo + h * LANES:lo + (h + 1) * LANES, :] = q_t.astype(BF16)
        k_ref[:, lo:hi] = (_dot(c_kv, wk_ref[:, lo:hi]) + k_rope2).astype(BF16)
    v_t = _dot_nt(wvt_ref[...], c_kv).astype(BF16)
    ones = jnp.ones((MLA_VT_ROWS - MLA_V_DIM, v_t.shape[1]), BF16)
    for h in range(N_MLA_HEADS):
        vt_ref[h * MLA_VT_ROWS:h * MLA_VT_ROWS + MLA_V_DIM, :] = v_t[h * MLA_V_DIM:(h + 1) * MLA_V_DIM, :]
        vt_ref[h * MLA_VT_ROWS + MLA_V_DIM:(h + 1) * MLA_VT_ROWS, :] = ones
    omem_ref[...] = _memory_attention(_dot(xb, wqm_ref[...]), km_ref, vm_ref).astype(BF16)


def _mla_inproj(x, w, cos, sin, km, vm):
    b, s, d = x.shape
    m = km.shape[1]
    tm = min(ROW_TILE, s)
    hw = N_MLA_HEADS * LANES
    mem = pl.BlockSpec((None, m, MEM_WIDTH), lambda bi, i: (bi, 0, 0))
    table = pl.BlockSpec((tm, LANES), lambda bi, i: (i, 0))
    table_t = pl.BlockSpec((LANES, tm), lambda bi, i: (0, i))
    rows = lambda width: pl.BlockSpec((None, tm, width), lambda bi, i: (bi, i, 0))
    cols = lambda width: pl.BlockSpec((None, width, tm), lambda bi, i: (bi, 0, i))
    weights = [w["cq"], w["ckv"], w["kr"], w["kr_rot"], w["qmem"], w["gq"], w["gkv"],
               w["q"].T, w["k"], w["v"].T]
    return pl.pallas_call(
        _mla_inproj_kernel,
        grid=(b, s // tm),
        in_specs=([rows(d)] + [_resident(a.shape) for a in weights]
                  + [table, table, table_t, table_t, mem, mem]),
        out_specs=[cols(hw), rows(hw), cols(N_MLA_HEADS * MLA_VT_ROWS), rows(MEM_WIDTH)],
        out_shape=[jax.ShapeDtypeStruct((b, hw, s), BF16), jax.ShapeDtypeStruct((b, s, hw), BF16),
                   jax.ShapeDtypeStruct((b, N_MLA_HEADS * MLA_VT_ROWS, s), BF16),
                   jax.ShapeDtypeStruct((b, s, MEM_WIDTH), BF16)],
        compiler_params=_params("parallel", "parallel"),
        name="mla_inproj",
    )(x, *weights, cos, sin, cos.T, sin.T, km, vm)


def _mla_attn_kernel(qt_ref, k_ref, vt_ref, o_ref, s0_ref, s1_ref, b0_ref, b1_ref, m_ref, acc_ref):
    t = qt_ref.shape[1]
    i = pl.program_id(2)
    m_ref[...] = jnp.full_like(m_ref, -jnp.inf)
    acc_ref[...] = jnp.zeros_like(acc_ref)

    def scores(j, s_ref, b_ref):
        start = pl.multiple_of(j * t, t)
        for h in range(2):
            tile = slice(h * LANES, (h + 1) * LANES)
            s_t = _dot(k_ref[pl.ds(start, t), tile], qt_ref[tile, :])
            s_ref[h] = s_t
            b_ref[h] = jnp.max(s_t, axis=0, keepdims=True)

    def update(j, s_ref, b_ref, masked):
        start = pl.multiple_of(j * t, t)
        for h in range(2):
            s_t = s_ref[h]
            if masked:
                key = lax.broadcasted_iota(jnp.int32, (t, t), 0)
                qry = lax.broadcasted_iota(jnp.int32, (t, t), 1)
                s_t = jnp.where(key <= qry, s_t, -jnp.inf)
                block_max = jnp.max(s_t, axis=0, keepdims=True)
            else:
                block_max = b_ref[h]
            m_old = m_ref[h]
            m_new = jnp.maximum(m_old, block_max)
            alpha = jnp.exp2(m_old - m_new)
            prob_t = jnp.exp2(s_t - m_new)
            v_t = vt_ref[h * MLA_VT_ROWS:(h + 1) * MLA_VT_ROWS, pl.ds(start, t)]
            acc_ref[h] = alpha * acc_ref[h] + _dot(v_t, prob_t.astype(BF16))
            m_ref[h] = m_new

    def two_full_steps(j):
        scores(j + 1, s1_ref, b1_ref)
        update(j, s0_ref, b0_ref, masked=False)
        scores(j + 2, s0_ref, b0_ref)
        update(j + 1, s1_ref, b1_ref, masked=False)

    def four_full_steps(jj, carry):
        two_full_steps(4 * jj)
        two_full_steps(4 * jj + 2)
        return carry

    scores(0, s0_ref, b0_ref)
    lax.fori_loop(0, i // 4, four_full_steps, 0)

    @pl.when(i % 4 >= 2)
    def _two_more_full_steps():
        two_full_steps(4 * (i // 4))

    @pl.when(i % 2 == 0)
    def _diagonal_in_s0():
        update(i, s0_ref, b0_ref, masked=True)

    @pl.when(i % 2 == 1)
    def _last_full_then_diagonal():
        scores(i, s1_ref, b1_ref)
        update(i - 1, s0_ref, b0_ref, masked=False)
        update(i, s1_ref, b1_ref, masked=True)

    heads = [acc_ref[h, :MLA_V_DIM, :] / acc_ref[h, MLA_V_DIM:MLA_V_DIM + 1, :] for h in range(2)]
    o_ref[...] = jnp.concatenate(heads, axis=0).T.astype(o_ref.dtype)


def _mla_attention(qt, k, vt):
    b, s, _ = k.shape
    t = min(MLA_TILE, s)
    return pl.pallas_call(
        _mla_attn_kernel,
        grid=(b, N_MLA_HEADS // 2, s // t),
        in_specs=[pl.BlockSpec((None, 2 * LANES, t), lambda bi, p, i: (bi, p, i)),
                  pl.BlockSpec((None, s, 2 * LANES), lambda bi, p, i: (bi, 0, p)),
                  pl.BlockSpec((None, 2 * MLA_VT_ROWS, s), lambda bi, p, i: (bi, p, 0))],
        out_specs=pl.BlockSpec((None, t, LANES), lambda bi, p, i: (bi, i, p)),
        out_shape=jax.ShapeDtypeStruct((b, s, MLA_WIDTH), BF16),
        scratch_shapes=[pltpu.VMEM((2, t, t), F32), pltpu.VMEM((2, t, t), F32),
                        pltpu.VMEM((2, 1, t), F32), pltpu.VMEM((2, 1, t), F32),
                        pltpu.VMEM((2, 1, t), F32), pltpu.VMEM((2, MLA_VT_ROWS, t), F32)],
        compiler_params=_params("parallel", "parallel", "arbitrary"),
        name="mla_attention",
    )(qt, k, vt)


def _outproj_ffn_kernel(x_ref, a_ref, m_ref, wo_ref, gm_ref, bm_ref, win_ref, wout_ref, g_ref, b_ref, o_ref):
    width = a_ref.shape[1]
    mixed = _dot(a_ref[...], wo_ref[:width, :]) + _dot(m_ref[...], wo_ref[width:, :])
    x = _layer_norm(DEEPNORM_ALPHA * x_ref[...] + mixed, gm_ref[...], bm_ref[...])
    o_ref[...] = _ffn_block(x, win_ref, wout_ref, g_ref, b_ref)


def _outproj_ffn(x, attn, omem, w_out, g_mix, b_mix, w_in, w_ffn_out, g, b):
    t, d = x.shape
    tm = min(ROW_TILE, t)
    rows = lambda width: pl.BlockSpec((tm, width), lambda i: (i, 0))
    vec = lambda a: a.reshape(1, d)
    return pl.pallas_call(
        _outproj_ffn_kernel,
        grid=(t // tm,),
        in_specs=[rows(d), rows(attn.shape[1]), rows(omem.shape[1]), _resident(w_out.shape),
                  _resident((1, d)), _resident((1, d)), _resident(w_in.shape),
                  _resident(w_ffn_out.shape), _resident((1, d)), _resident((1, d))],
        out_specs=rows(d),
        out_shape=jax.ShapeDtypeStruct((t, d), F32),
        compiler_params=_params("parallel"),
        name="outproj_ffn",
    )(x, attn, omem, w_out, vec(g_mix), vec(b_mix), w_in, w_ffn_out, vec(g), vec(b))


def _rotate_half_columns(w):
    half = w.shape[1] // 2
    return jnp.concatenate([-w[:, half:], w[:, :half]], axis=1)


def _head_tiles(cols_per_head, offset):
    k, h, c = cols_per_head.shape
    tiles = jnp.zeros((k, h, LANES), cols_per_head.dtype).at[:, :, offset:offset + c].set(cols_per_head)
    return tiles.reshape(k, h * LANES)


def _mla_weights(w_in, q_norm_g, w_uq, kv_norm_g, w_ukv):
    q_dim = MLA_NOPE_DIM + MLA_ROPE_DIM
    kr0 = MLA_Q_RANK + MLA_KV_RANK
    w_kr = w_in[:, kr0:kr0 + MLA_ROPE_DIM]
    uq = w_uq.reshape(MLA_Q_RANK, N_MLA_HEADS, q_dim)
    ukv = w_ukv.reshape(MLA_KV_RANK, N_MLA_HEADS, MLA_NOPE_DIM + MLA_V_DIM)
    as_tile = lambda w, off: _head_tiles(w[:, None, :], off)
    w = {
        "cq": w_in[:, :MLA_Q_RANK],
        "ckv": w_in[:, MLA_Q_RANK:kr0],
        "kr": as_tile(w_kr, MLA_NOPE_DIM),
        "kr_rot": as_tile(_rotate_half_columns(w_kr), MLA_NOPE_DIM),
        "qmem": w_in[:, kr0 + MLA_ROPE_DIM:],
        "q": _head_tiles(uq, 0),
        "k": _head_tiles(ukv[:, :, :MLA_NOPE_DIM], 0),
        "v": ukv[:, :, MLA_NOPE_DIM:].reshape(MLA_KV_RANK, MLA_WIDTH),
    }
    w = {name: a.astype(BF16) for name, a in w.items()}
    w["gq"] = q_norm_g.reshape(1, MLA_Q_RANK)
    w["gkv"] = kv_norm_g.reshape(1, MLA_KV_RANK)
    return w


def _rope_tables(s):
    freqs = ROPE_BASE ** (-jnp.arange(0, MLA_ROPE_DIM, 2, dtype=F32) / MLA_ROPE_DIM)
    ang = jnp.arange(s).astype(F32)[:, None] * freqs[None, :]
    pad = jnp.zeros((s, LANES - MLA_NOPE_DIM - MLA_ROPE_DIM), F32)
    cos = jnp.concatenate([jnp.ones((s, MLA_NOPE_DIM), F32), jnp.cos(ang), jnp.cos(ang), pad], axis=1)
    sin = jnp.concatenate([jnp.zeros((s, MLA_NOPE_DIM), F32), jnp.sin(ang), jnp.sin(ang), pad], axis=1)
    return cos, sin


def kernel(x, mem, ln_ffn1_g, ln_ffn1_b, ln_mix_g, ln_mix_b, ln_ffn2_g, ln_ffn2_b, ffn1_w_in, ffn1_w_out, ffn2_w_in, ffn2_w_out, sb_w_in, mla_w_in, mla_q_norm_g, mla_w_uq, mla_kv_norm_g, mla_w_ukv, mem_w_kv, w_out):
    b, s, d = x.shape
    flat = lambda a: a.reshape(b * s, a.shape[-1])
    cos, sin = _rope_tables(s)
    for i in range(DEPTH):
        x = _ffn_ln(flat(x), ffn1_w_in[i].astype(BF16), ffn1_w_out[i].astype(BF16),
                    ln_ffn1_g[i], ln_ffn1_b[i]).reshape(b, s, d)
        km, vm = _mem_kv(mem, mem_w_kv[i].astype(BF16))
        j = i // 2
        if i % 2 == 0:
            qt, k, vt, omem = _sb_inproj(x, sb_w_in[j].astype(BF16), km, vm)
            attn = _sb_attention(qt, k, vt)
        else:
            w = _mla_weights(mla_w_in[j], mla_q_norm_g[j], mla_w_uq[j], mla_kv_norm_g[j], mla_w_ukv[j])
            qt, k, vt, omem = _mla_inproj(x, w, cos, sin, km, vm)
            attn = _mla_attention(qt, k, vt)
        x = _outproj_ffn(flat(x), flat(attn), flat(omem), w_out[i].astype(BF16), ln_mix_g[i], ln_mix_b[i],
                         ffn2_w_in[i].astype(BF16), ffn2_w_out[i].astype(BF16),
                         ln_ffn2_g[i], ln_ffn2_b[i]).reshape(b, s, d)
    return x
```

```python
import math

import jax
import jax.numpy as jnp
from jax import lax
from jax.experimental import pallas as pl
from jax.experimental.pallas import tpu as pltpu

D_MODEL = 1024
DEPTH = 2
HEAD_DIM = 64
N_SB_HEADS = 12
N_MLA_HEADS = 12
MLA_NOPE_DIM = 64
MLA_ROPE_DIM = 32
MLA_V_DIM = 64
MLA_Q_RANK = 384
MLA_KV_RANK = 256
ROPE_BASE = 10000.0
N_MEM_HEADS = 4
MEM_HEAD_DIM = 64
MEM_WIDTH = N_MEM_HEADS * MEM_HEAD_DIM
SB_WIDTH = N_SB_HEADS * HEAD_DIM
MLA_WIDTH = N_MLA_HEADS * MLA_V_DIM
D_FF = 2816
LN_EPS = 1e-5
RMS_EPS = 1e-6
DEEPNORM_ALPHA = (2 * DEPTH) ** 0.25

LANES = 128
VMEM_LIMIT_BYTES = 56 * 1024 * 1024

ROW_TILE = 512
PROJ_ROW_TILE = 1024
MXU_TILE = 256
FFN_SPLIT = (0, 6 * MXU_TILE, D_FF)
BF16_ROWS = 16
MLA_VT_ROWS = MLA_V_DIM + BF16_ROWS
SB_QUERY_TILE = 2048
SB_SUB = LANES
SB_WINDOW = 2 * SB_SUB
MLA_TILE = 512
MLA_UNROLL = 8
SB_LOG_WEIGHT_FLOOR = -88.0
MLA_LOG2_SCALE = math.log2(math.e) / math.sqrt(MLA_NOPE_DIM + MLA_ROPE_DIM)

BF16 = jnp.bfloat16
F32 = jnp.float32


def _dot(a, b):
    return jnp.dot(a, b, preferred_element_type=F32)


def _dot_nt(a, b):
    return lax.dot_general(a, b, (((1,), (1,)), ((), ())), preferred_element_type=F32)


def _layer_norm(y, g, b):
    mu = jnp.mean(y, axis=-1, keepdims=True)
    d = y - mu
    var = jnp.mean(d * d, axis=-1, keepdims=True)
    return d * lax.rsqrt(var + LN_EPS) * g + b


def _rms_norm(x, g):
    return x * lax.rsqrt(jnp.mean(x * x, axis=-1, keepdims=True) + RMS_EPS) * g


def _first_head(shape, axis):
    return lax.broadcasted_iota(jnp.int32, shape, axis) < HEAD_DIM


def _params(*semantics):
    return pltpu.CompilerParams(dimension_semantics=semantics, vmem_limit_bytes=VMEM_LIMIT_BYTES)


def _resident(shape):
    nd = len(shape)
    return pl.BlockSpec(shape, lambda *_: (0,) * nd, pipeline_mode=pl.Buffered(1))


def _ffn_block(x, win_ref, wout_ref, g_ref, b_ref):
    xb = x.astype(BF16)
    acc = None
    for lo, hi in zip(FFN_SPLIT[:-1], FFN_SPLIT[1:]):
        gate = _dot(xb, win_ref[:, lo:hi])
        up = _dot(xb, win_ref[:, D_FF + lo:D_FF + hi])
        act = (gate * jax.nn.sigmoid(gate) * up).astype(BF16)
        part = _dot(act, wout_ref[lo:hi, :])
        acc = part if acc is None else acc + part
    return _layer_norm(DEEPNORM_ALPHA * x + 0.5 * acc, g_ref[...], b_ref[...])


def _ffn_ln_kernel(x_ref, win_ref, wout_ref, g_ref, b_ref, o_ref):
    o_ref[...] = _ffn_block(x_ref[...], win_ref, wout_ref, g_ref, b_ref)


def _ffn_ln(x, w_in, w_out, g, b):
    t, d = x.shape
    tm = min(ROW_TILE, t)
    row = pl.BlockSpec((tm, d), lambda i: (i, 0))
    return pl.pallas_call(
        _ffn_ln_kernel,
        grid=(t // tm,),
        in_specs=[row, _resident(w_in.shape), _resident(w_out.shape),
                  _resident((1, d)), _resident((1, d))],
        out_specs=row,
        out_shape=jax.ShapeDtypeStruct((t, d), F32),
        compiler_params=_params("parallel"),
        name="ffn_ln",
    )(x, w_in, w_out, g.reshape(1, d), b.reshape(1, d))


def _mem_kv_kernel(mem_ref, w_ref, k_ref, v_ref):
    kv = _dot(mem_ref[...].astype(BF16), w_ref[...])
    k_ref[...] = kv[:, :MEM_WIDTH].astype(BF16)
    v_ref[...] = kv[:, MEM_WIDTH:].astype(BF16)


def _mem_kv(mem, w_kv):
    b, m, d = mem.shape
    out = pl.BlockSpec((None, m, MEM_WIDTH), lambda i: (i, 0, 0))
    return pl.pallas_call(
        _mem_kv_kernel,
        grid=(b,),
        in_specs=[pl.BlockSpec((None, m, d), lambda i: (i, 0, 0)), _resident(w_kv.shape)],
        out_specs=[out, out],
        out_shape=[jax.ShapeDtypeStruct((b, m, MEM_WIDTH), BF16)] * 2,
        compiler_params=_params("parallel"),
        name="mem_kv",
    )(mem, w_kv)


def _memory_attention(q_mem, km_ref, vm_ref):
    tm = q_mem.shape[0]
    low = _first_head((tm, LANES), 1)
    scale = 1.0 / math.sqrt(MEM_HEAD_DIM)
    outs = []
    for p in range(MEM_WIDTH // LANES):
        q2 = q_mem[:, p * LANES:(p + 1) * LANES]
        k2 = km_ref[:, p * LANES:(p + 1) * LANES]
        v2 = vm_ref[:, p * LANES:(p + 1) * LANES]
        pair = None
        for own in (low, jnp.logical_not(low)):
            qh = jnp.where(own, q2, 0.0).astype(BF16)
            s = _dot_nt(qh, k2) * scale
            e = jnp.exp(s - jnp.max(s, axis=-1, keepdims=True))
            prob = e / jnp.sum(e, axis=-1, keepdims=True)
            o = _dot(prob.astype(BF16), v2)
            pair = o if pair is None else jnp.where(low, pair, o)
        outs.append(pair)
    return jnp.concatenate(outs, axis=-1)


def _sb_inproj_kernel(x_ref, wqt_ref, wk_ref, wvt_ref, wqm_ref, km_ref, vm_ref,
                      qt_ref, k_ref, vt_ref, omem_ref):
    xb = x_ref[...].astype(BF16)
    qt_ref[...] = (_dot_nt(wqt_ref[...], xb) * (1.0 / math.sqrt(HEAD_DIM))).astype(BF16)
    k_ref[...] = _dot(xb, wk_ref[...]).astype(BF16)
    vt_ref[...] = _dot_nt(wvt_ref[...], xb).astype(BF16)
    omem_ref[...] = _memory_attention(_dot(xb, wqm_ref[...]), km_ref, vm_ref).astype(BF16)


def _sb_inproj(x, w_in, km, vm):
    b, s, d = x.shape
    m = km.shape[1]
    tm = min(PROJ_ROW_TILE, s)
    w = SB_WIDTH
    weights = [w_in[:, :w].T, w_in[:, w:2 * w], w_in[:, 2 * w:3 * w].T, w_in[:, 3 * w:]]
    mem = pl.BlockSpec((None, m, MEM_WIDTH), lambda bi, i: (bi, 0, 0))
    rows = lambda width: pl.BlockSpec((None, tm, width), lambda bi, i: (bi, i, 0))
    cols = pl.BlockSpec((None, w, tm), lambda bi, i: (bi, 0, i))
    return pl.pallas_call(
        _sb_inproj_kernel,
        grid=(b, s // tm),
        in_specs=[rows(d)] + [_resident(a.shape) for a in weights] + [mem, mem],
        out_specs=[cols, rows(w), cols, rows(MEM_WIDTH)],
        out_shape=[jax.ShapeDtypeStruct((b, w, s), BF16), jax.ShapeDtypeStruct((b, s, w), BF16),
                   jax.ShapeDtypeStruct((b, w, s), BF16), jax.ShapeDtypeStruct((b, s, MEM_WIDTH), BF16)],
        compiler_params=_params("parallel", "parallel"),
        name="sb_inproj",
    )(x, *weights, km, vm)


def _split3(x):
    hi = x.astype(BF16)
    r = x - hi.astype(F32)
    mid = r.astype(BF16)
    lo = (r - mid.astype(F32)).astype(BF16)
    return hi, mid, lo


def _later3(n):
    row = lax.broadcasted_iota(jnp.int32, (n, n), 0)
    col = lax.broadcasted_iota(jnp.int32, (n, n), 1)
    later = jnp.where(col > row, 1.0, 0.0).astype(BF16)
    return jnp.concatenate([later, later, later], axis=1)


def _sb_logs(z, visible):
    decay = jnp.log(1.0 + jnp.exp2(jnp.abs(z) * -math.log2(math.e))) + jnp.maximum(z, 0.0)
    return z - decay, jnp.where(visible, decay, 0.0)


def _sb_behind(later3, decay):
    return _dot(later3, jnp.concatenate(_split3(decay), axis=0))


def _sb_chain(qt, k_ref, vt_ref, start, n_keys, first_query, tail_in):
    key = lax.broadcasted_iota(jnp.int32, (n_keys, SB_SUB), 0)
    qry = lax.broadcasted_iota(jnp.int32, (n_keys, SB_SUB), 1)
    z = _dot(k_ref[pl.ds(start, n_keys), :], qt)
    visible = (key - qry) < (first_query - start)
    log_beta, decay = _sb_logs(z, visible)
    behind = _sb_behind(_later3(n_keys), decay)
    w = jnp.where(visible, jnp.exp(log_beta - behind + tail_in), 0.0)
    out_t = _dot(vt_ref[:, pl.ds(start, n_keys)], w.astype(BF16))
    return out_t, tail_in - jnp.sum(decay, axis=0, keepdims=True)


def _sb_attn_kernel(qt_ref, k_ref, vt_ref, o_ref, acc_ref, tail_ref):
    tq = qt_ref.shape[1]
    n_sub = tq // SB_SUB
    q0 = pl.program_id(2) * tq
    first = _first_head((LANES, SB_SUB), 0)
    owners = (first, jnp.logical_not(first))
    subs = [slice(u * SB_SUB, (u + 1) * SB_SUB) for u in range(n_sub)]

    def head_queries(u, h):
        return jnp.where(owners[h], qt_ref[:, subs[u]].astype(F32), 0.0).astype(BF16)

    starts = [pl.multiple_of(jnp.maximum(q0 + (u - 1) * SB_SUB, 0), SB_SUB) for u in range(n_sub)]
    later3 = _later3(SB_WINDOW)
    key = lax.broadcasted_iota(jnp.int32, (SB_WINDOW, 2 * SB_SUB), 0)
    qry = lax.broadcasted_iota(jnp.int32, (SB_WINDOW, 2 * SB_SUB), 1) & (SB_SUB - 1)
    ahead = key - qry
    zs = [_dot(k_ref[pl.ds(starts[u], SB_WINDOW), :],
               jnp.concatenate([head_queries(u, 0), head_queries(u, 1)], axis=1)) for u in range(n_sub)]
    stage = []
    for u in range(n_sub):
        visible = ahead < (q0 + u * SB_SUB - starts[u])
        log_beta, decay = _sb_logs(zs[u], visible)
        stage.append((visible, log_beta, decay, _sb_behind(later3, decay)))
    pending = jnp.full((1, SB_SUB), -jnp.inf, F32)
    for u in range(n_sub):
        visible, log_beta, decay, behind = stage[u]
        w = jnp.where(visible, jnp.exp(log_beta - behind), 0.0)
        out_t = _dot(vt_ref[:, pl.ds(starts[u], SB_WINDOW)], w.astype(BF16))
        tail = -jnp.sum(decay, axis=0, keepdims=True)
        for h in range(2):
            acc_ref[h, :, subs[u]] = out_t[:, h * SB_SUB:(h + 1) * SB_SUB]
            tail_ref[h, :, subs[u]] = tail[:, h * SB_SUB:(h + 1) * SB_SUB]
        tail_max = jnp.maximum(tail[:, :SB_SUB], tail[:, SB_SUB:])
        pending = jnp.maximum(pending, jnp.where(starts[u] > 0, tail_max, -jnp.inf))

    @pl.when(jnp.max(pending) > SB_LOG_WEIGHT_FLOOR)
    def _walk_further_back():
        for u in range(n_sub):
            for h in range(2):
                qt = head_queries(u, h)

                def cond(state):
                    start, tail_max = state
                    return jnp.logical_and(start > 0, tail_max > SB_LOG_WEIGHT_FLOOR)

                def body(state, qt=qt, u=u, h=h):
                    start = pl.multiple_of(state[0] - SB_SUB, SB_SUB)
                    out_t, tail = _sb_chain(qt, k_ref, vt_ref, start, SB_SUB,
                                            q0 + u * SB_SUB, tail_ref[h, :, subs[u]])
                    acc_ref[h, :, subs[u]] += out_t
                    tail_ref[h, :, subs[u]] = tail
                    return start, jnp.max(tail)

                lax.while_loop(cond, body, (starts[u], jnp.max(tail_ref[h, :, subs[u]])))

    out_t = jnp.where(_first_head((LANES, tq), 0), acc_ref[0], acc_ref[1])
    o_ref[...] = out_t.T.astype(o_ref.dtype)


def _sb_attention(qt, k, vt):
    b, s, _ = k.shape
    tq = min(SB_QUERY_TILE, s)
    pairs = SB_WIDTH // LANES
    return pl.pallas_call(
        _sb_attn_kernel,
        grid=(b, pairs, s // tq),
        in_specs=[pl.BlockSpec((None, LANES, tq), lambda bi, p, i: (bi, p, i)),
                  pl.BlockSpec((None, s, LANES), lambda bi, p, i: (bi, 0, p)),
                  pl.BlockSpec((None, LANES, s), lambda bi, p, i: (bi, p, 0))],
        out_specs=pl.BlockSpec((None, tq, LANES), lambda bi, p, i: (bi, i, p)),
        out_shape=jax.ShapeDtypeStruct((b, s, SB_WIDTH), BF16),
        scratch_shapes=[pltpu.VMEM((2, LANES, tq), F32), pltpu.VMEM((2, 1, tq), F32)],
        compiler_params=_params("parallel", "parallel", "arbitrary"),
        name="sb_attention",
    )(qt, k, vt)


def _mla_inproj_kernel(x_ref, wcq_ref, wckv_ref, wkr_ref, wkrr_ref, wqm_ref, gq_ref, gkv_ref,
                       wqt_ref, wk_ref, wvt_ref, cos_ref, sin_ref, cost_ref, sint_ref,
                       km_ref, vm_ref, qt_ref, k_ref, vt_ref, omem_ref):
    xb = x_ref[...].astype(BF16)
    cos, sin = cos_ref[...], sin_ref[...]
    cos_t, sin_t = cost_ref[...], sint_ref[...]
    c_q = _rms_norm(_dot(xb, wcq_ref[...]), gq_ref[...]).astype(BF16)
    c_kv = _rms_norm(_dot(xb, wckv_ref[...]), gkv_ref[...]).astype(BF16)
    k_rope = _dot(xb, wkr_ref[...]) * cos + _dot(xb, wkrr_ref[...]) * sin
    k_rope2 = jnp.concatenate([k_rope, k_rope], axis=-1)
    r0, r1, r2 = MLA_NOPE_DIM, MLA_NOPE_DIM + MLA_ROPE_DIM // 2, MLA_NOPE_DIM + MLA_ROPE_DIM
    for p in range(N_MLA_HEADS // 2):
        lo, hi = 2 * p * LANES, 2 * (p + 1) * LANES
        q_pair = _dot_nt(wqt_ref[lo:hi, :], c_q)
        for h in range(2):
            q_t = q_pair[h * LANES:(h + 1) * LANES]
            rotated = jnp.concatenate([q_t[:r0], -q_t[r1:r2], q_t[r0:r1], q_t[r2:]], axis=0)
            q_t = (q_t * cos_t + rotated * sin_t) * MLA_LOG2_SCALE
            qt_ref[lo + h * LANES:lo + (h + 1) * LANES, :] = q_t.astype(BF16)
        k_ref[:, lo:hi] = (_dot(c_kv, wk_ref[:, lo:hi]) + k_rope2).astype(BF16)
    v_t = _dot_nt(wvt_ref[...], c_kv).astype(BF16)
    ones = jnp.ones((MLA_VT_ROWS - MLA_V_DIM, v_t.shape[1]), BF16)
    for h in range(N_MLA_HEADS):
        vt_ref[h * MLA_VT_ROWS:h * MLA_VT_ROWS + MLA_V_DIM, :] = v_t[h * MLA_V_DIM:(h + 1) * MLA_V_DIM, :]
        vt_ref[h * MLA_VT_ROWS + MLA_V_DIM:(h + 1) * MLA_VT_ROWS, :] = ones
    omem_ref[...] = _memory_attention(_dot(xb, wqm_ref[...]), km_ref, vm_ref).astype(BF16)


def _mla_inproj(x, w, cos, sin, km, vm):
    b, s, d = x.shape
    m = km.shape[1]
    tm = min(PROJ_ROW_TILE, s)
    hw = N_MLA_HEADS * LANES
    mem = pl.BlockSpec((None, m, MEM_WIDTH), lambda bi, i: (bi, 0, 0))
    table = pl.BlockSpec((tm, LANES), lambda bi, i: (i, 0))
    table_t = pl.BlockSpec((LANES, tm), lambda bi, i: (0, i))
    rows = lambda width: pl.BlockSpec((None, tm, width), lambda bi, i: (bi, i, 0))
    cols = lambda width: pl.BlockSpec((None, width, tm), lambda bi, i: (bi, 0, i))
    weights = [w["cq"], w["ckv"], w["kr"], w["kr_rot"], w["qmem"], w["gq"], w["gkv"],
               w["q"].T, w["k"], w["v"].T]
    return pl.pallas_call(
        _mla_inproj_kernel,
        grid=(b, s // tm),
        in_specs=([rows(d)] + [_resident(a.shape) for a in weights]
                  + [table, table, table_t, table_t, mem, mem]),
        out_specs=[cols(hw), rows(hw), cols(N_MLA_HEADS * MLA_VT_ROWS), rows(MEM_WIDTH)],
        out_shape=[jax.ShapeDtypeStruct((b, hw, s), BF16), jax.ShapeDtypeStruct((b, s, hw), BF16),
                   jax.ShapeDtypeStruct((b, N_MLA_HEADS * MLA_VT_ROWS, s), BF16),
                   jax.ShapeDtypeStruct((b, s, MEM_WIDTH), BF16)],
        compiler_params=_params("parallel", "parallel"),
        name="mla_inproj",
    )(x, *weights, cos, sin, cos.T, sin.T, km, vm)


def _mla_attn_kernel(qt_ref, k_ref, vt_ref, o_ref, s0_ref, s1_ref, b0_ref, b1_ref, m_ref, acc_ref):
    t = qt_ref.shape[1]
    i = pl.program_id(2)
    m_ref[...] = jnp.full_like(m_ref, -jnp.inf)
    acc_ref[...] = jnp.zeros_like(acc_ref)

    def scores(j, s_ref, b_ref):
        start = pl.multiple_of(j * t, t)
        for h in range(2):
            tile = slice(h * LANES, (h + 1) * LANES)
            s_t = _dot(k_ref[pl.ds(start, t), tile], qt_ref[tile, :])
            s_ref[h] = s_t
            b_ref[h] = jnp.max(s_t, axis=0, keepdims=True)

    def update(j, s_ref, b_ref, masked):
        start = pl.multiple_of(j * t, t)
        for h in range(2):
            s_t = s_ref[h]
            if masked:
                key = lax.broadcasted_iota(jnp.int32, (t, t), 0)
                qry = lax.broadcasted_iota(jnp.int32, (t, t), 1)
                s_t = jnp.where(key <= qry, s_t, -jnp.inf)
                block_max = jnp.max(s_t, axis=0, keepdims=True)
            else:
                block_max = b_ref[h]
            m_old = m_ref[h]
            m_new = jnp.maximum(m_old, block_max)
            alpha = jnp.exp2(m_old - m_new)
            prob_t = jnp.exp2(s_t - m_new)
            v_t = vt_ref[h * MLA_VT_ROWS:(h + 1) * MLA_VT_ROWS, pl.ds(start, t)]
            acc_ref[h] = alpha * acc_ref[h] + _dot(v_t, prob_t.astype(BF16))
            m_ref[h] = m_new

    def two_full_steps(j):
        scores(j + 1, s1_ref, b1_ref)
        update(j, s0_ref, b0_ref, masked=False)
        scores(j + 2, s0_ref, b0_ref)
        update(j + 1, s1_ref, b1_ref, masked=False)

    def eight_full_steps(jj, carry):
        for step in range(0, MLA_UNROLL, 2):
            two_full_steps(MLA_UNROLL * jj + step)
        return carry

    scores(0, s0_ref, b0_ref)
    lax.fori_loop(0, i // MLA_UNROLL, eight_full_steps, 0)
    span = MLA_UNROLL // 2
    while span >= 2:
        @pl.when(i % (2 * span) >= span)
        def _more_full_steps(span=span):
            for step in range(0, span, 2):
                two_full_steps((2 * span) * (i // (2 * span)) + step)
        span //= 2

    @pl.when(i % 2 == 0)
    def _diagonal_in_s0():
        update(i, s0_ref, b0_ref, masked=True)

    @pl.when(i % 2 == 1)
    def _last_full_then_diagonal():
        scores(i, s1_ref, b1_ref)
        update(i - 1, s0_ref, b0_ref, masked=False)
        update(i, s1_ref, b1_ref, masked=True)

    heads = [acc_ref[h, :MLA_V_DIM, :] / acc_ref[h, MLA_V_DIM:MLA_V_DIM + 1, :] for h in range(2)]
    o_ref[...] = jnp.concatenate(heads, axis=0).T.astype(o_ref.dtype)


def _mla_attention(qt, k, vt):
    b, s, _ = k.shape
    t = min(MLA_TILE, s)
    return pl.pallas_call(
        _mla_attn_kernel,
        grid=(b, N_MLA_HEADS // 2, s // t),
        in_specs=[pl.BlockSpec((None, 2 * LANES, t), lambda bi, p, i: (bi, p, i)),
                  pl.BlockSpec((None, s, 2 * LANES), lambda bi, p, i: (bi, 0, p)),
                  pl.BlockSpec((None, 2 * MLA_VT_ROWS, s), lambda bi, p, i: (bi, p, 0))],
        out_specs=pl.BlockSpec((None, t, LANES), lambda bi, p, i: (bi, i, p)),
        out_shape=jax.ShapeDtypeStruct((b, s, MLA_WIDTH), BF16),
        scratch_shapes=[pltpu.VMEM((2, t, t), F32), pltpu.VMEM((2, t, t), F32),
                        pltpu.VMEM((2, 1, t), F32), pltpu.VMEM((2, 1, t), F32),
                        pltpu.VMEM((2, 1, t), F32), pltpu.VMEM((2, MLA_VT_ROWS, t), F32)],
        compiler_params=_params("parallel", "parallel", "arbitrary"),
        name="mla_attention",
    )(qt, k, vt)


def _outproj_ffn_kernel(x_ref, a_ref, m_ref, wo_ref, gm_ref, bm_ref, win_ref, wout_ref, g_ref, b_ref, o_ref):
    width = a_ref.shape[1]
    mixed = _dot(a_ref[...], wo_ref[:width, :]) + _dot(m_ref[...], wo_ref[width:, :])
    x = _layer_norm(DEEPNORM_ALPHA * x_ref[...] + mixed, gm_ref[...], bm_ref[...])
    o_ref[...] = _ffn_block(x, win_ref, wout_ref, g_ref, b_ref)


def _outproj_ffn(x, attn, omem, w_out, g_mix, b_mix, w_in, w_ffn_out, g, b):
    t, d = x.shape
    tm = min(ROW_TILE, t)
    rows = lambda width: pl.BlockSpec((tm, width), lambda i: (i, 0))
    vec = lambda a: a.reshape(1, d)
    return pl.pallas_call(
        _outproj_ffn_kernel,
        grid=(t // tm,),
        in_specs=[rows(d), rows(attn.shape[1]), rows(omem.shape[1]), _resident(w_out.shape),
                  _resident((1, d)), _resident((1, d)), _resident(w_in.shape),
                  _resident(w_ffn_out.shape), _resident((1, d)), _resident((1, d))],
        out_specs=rows(d),
        out_shape=jax.ShapeDtypeStruct((t, d), F32),
        compiler_params=_params("parallel"),
        name="outproj_ffn",
    )(x, attn, omem, w_out, vec(g_mix), vec(b_mix), w_in, w_ffn_out, vec(g), vec(b))


def _rotate_half_columns(w):
    half = w.shape[1] // 2
    return jnp.concatenate([-w[:, half:], w[:, :half]], axis=1)


def _head_tiles(cols_per_head, offset):
    k, h, c = cols_per_head.shape
    tiles = jnp.zeros((k, h, LANES), cols_per_head.dtype).at[:, :, offset:offset + c].set(cols_per_head)
    return tiles.reshape(k, h * LANES)


def _mla_weights(w_in, q_norm_g, w_uq, kv_norm_g, w_ukv):
    q_dim = MLA_NOPE_DIM + MLA_ROPE_DIM
    kr0 = MLA_Q_RANK + MLA_KV_RANK
    w_kr = w_in[:, kr0:kr0 + MLA_ROPE_DIM]
    uq = w_uq.reshape(MLA_Q_RANK, N_MLA_HEADS, q_dim)
    ukv = w_ukv.reshape(MLA_KV_RANK, N_MLA_HEADS, MLA_NOPE_DIM + MLA_V_DIM)
    as_tile = lambda w, off: _head_tiles(w[:, None, :], off)
    w = {
        "cq": w_in[:, :MLA_Q_RANK],
        "ckv": w_in[:, MLA_Q_RANK:kr0],
        "kr": as_tile(w_kr, MLA_NOPE_DIM),
        "kr_rot": as_tile(_rotate_half_columns(w_kr), MLA_NOPE_DIM),
        "qmem": w_in[:, kr0 + MLA_ROPE_DIM:],
        "q": _head_tiles(uq, 0),
        "k": _head_tiles(ukv[:, :, :MLA_NOPE_DIM], 0),
        "v": ukv[:, :, MLA_NOPE_DIM:].reshape(MLA_KV_RANK, MLA_WIDTH),
    }
    w = {name: a.astype(BF16) for name, a in w.items()}
    w["gq"] = q_norm_g.reshape(1, MLA_Q_RANK)
    w["gkv"] = kv_norm_g.reshape(1, MLA_KV_RANK)
    return w


def _rope_tables(s):
    freqs = ROPE_BASE ** (-jnp.arange(0, MLA_ROPE_DIM, 2, dtype=F32) / MLA_ROPE_DIM)
    ang = jnp.arange(s).astype(F32)[:, None] * freqs[None, :]
    pad = jnp.zeros((s, LANES - MLA_NOPE_DIM - MLA_ROPE_DIM), F32)
    cos = jnp.concatenate([jnp.ones((s, MLA_NOPE_DIM), F32), jnp.cos(ang), jnp.cos(ang), pad], axis=1)
    sin = jnp.concatenate([jnp.zeros((s, MLA_NOPE_DIM), F32), jnp.sin(ang), jnp.sin(ang), pad], axis=1)
    return cos, sin


def kernel(x, mem, ln_ffn1_g, ln_ffn1_b, ln_mix_g, ln_mix_b, ln_ffn2_g, ln_ffn2_b, ffn1_w_in, ffn1_w_out, ffn2_w_in, ffn2_w_out, sb_w_in, mla_w_in, mla_q_norm_g, mla_w_uq, mla_kv_norm_g, mla_w_ukv, mem_w_kv, w_out):
    b, s, d = x.shape
    flat = lambda a: a.reshape(b * s, a.shape[-1])
    cos, sin = _rope_tables(s)
    for i in range(DEPTH):
        x = _ffn_ln(flat(x), ffn1_w_in[i].astype(BF16), ffn1_w_out[i].astype(BF16),
                    ln_ffn1_g[i], ln_ffn1_b[i]).reshape(b, s, d)
        km, vm = _mem_kv(mem, mem_w_kv[i].astype(BF16))
        j = i // 2
        if i % 2 == 0:
            qt, k, vt, omem = _sb_inproj(x, sb_w_in[j].astype(BF16), km, vm)
            attn = _sb_attention(qt, k, vt)
        else:
            w = _mla_weights(mla_w_in[j], mla_q_norm_g[j], mla_w_uq[j], mla_kv_norm_g[j], mla_w_ukv[j])
            qt, k, vt, omem = _mla_inproj(x, w, cos, sin, km, vm)
            attn = _mla_attention(qt, k, vt)
        x = _outproj_ffn(flat(x), flat(attn), flat(omem), w_out[i].astype(BF16), ln_mix_g[i], ln_mix_b[i],
                         ffn2_w_in[i].astype(BF16), ffn2_w_out[i].astype(BF16),
                         ln_ffn2_g[i], ln_ffn2_b[i]).reshape(b, s, d)
    return x
```

```python
import math

import jax
import jax.numpy as jnp
from jax import lax
from jax.experimental import pallas as pl
from jax.experimental.pallas import tpu as pltpu

D_MODEL = 1024
DEPTH = 2
HEAD_DIM = 64
N_SB_HEADS = 12
N_MLA_HEADS = 12
MLA_NOPE_DIM = 64
MLA_ROPE_DIM = 32
MLA_V_DIM = 64
MLA_Q_RANK = 384
MLA_KV_RANK = 256
ROPE_BASE = 10000.0
N_MEM_HEADS = 4
MEM_HEAD_DIM = 64
MEM_WIDTH = N_MEM_HEADS * MEM_HEAD_DIM
SB_WIDTH = N_SB_HEADS * HEAD_DIM
MLA_WIDTH = N_MLA_HEADS * MLA_V_DIM
D_FF = 2816
LN_EPS = 1e-5
RMS_EPS = 1e-6
DEEPNORM_ALPHA = (2 * DEPTH) ** 0.25

LANES = 128
VMEM_LIMIT_BYTES = 56 * 1024 * 1024

ROW_TILE = 512
PROJ_ROW_TILE = 1024
MXU_TILE = 256
FFN_SPLIT = (0, 6 * MXU_TILE, D_FF)
BF16_ROWS = 16
MLA_VT_ROWS = MLA_V_DIM + BF16_ROWS
SB_QUERY_TILE = 2048
SB_SUB = LANES
SB_WINDOW = 2 * SB_SUB
MLA_TILE = 512
MLA_UNROLL = 8
SB_LOG_WEIGHT_FLOOR = -88.0
MLA_LOG2_SCALE = math.log2(math.e) / math.sqrt(MLA_NOPE_DIM + MLA_ROPE_DIM)

BF16 = jnp.bfloat16
F32 = jnp.float32


def _dot(a, b):
    return jnp.dot(a, b, preferred_element_type=F32)


def _dot_nt(a, b):
    return lax.dot_general(a, b, (((1,), (1,)), ((), ())), preferred_element_type=F32)


def _layer_norm(y, g, b):
    mu = jnp.mean(y, axis=-1, keepdims=True)
    d = y - mu
    var = jnp.mean(d * d, axis=-1, keepdims=True)
    return d * lax.rsqrt(var + LN_EPS) * g + b


def _rms_norm(x, g):
    return x * lax.rsqrt(jnp.mean(x * x, axis=-1, keepdims=True) + RMS_EPS) * g


def _first_head(shape, axis):
    return lax.broadcasted_iota(jnp.int32, shape, axis) < HEAD_DIM


def _params(*semantics):
    return pltpu.CompilerParams(dimension_semantics=semantics, vmem_limit_bytes=VMEM_LIMIT_BYTES)


def _resident(shape):
    nd = len(shape)
    return pl.BlockSpec(shape, lambda *_: (0,) * nd, pipeline_mode=pl.Buffered(1))


def _ffn_block(x, win_ref, wout_ref, g_ref, b_ref):
    xb = x.astype(BF16)
    acc = None
    for lo, hi in zip(FFN_SPLIT[:-1], FFN_SPLIT[1:]):
        gate = _dot(xb, win_ref[:, lo:hi])
        up = _dot(xb, win_ref[:, D_FF + lo:D_FF + hi])
        act = (gate * jax.nn.sigmoid(gate) * up).astype(BF16)
        part = _dot(act, wout_ref[lo:hi, :])
        acc = part if acc is None else acc + part
    return _layer_norm(DEEPNORM_ALPHA * x + 0.5 * acc, g_ref[...], b_ref[...])


def _ffn_ln_kernel(x_ref, win_ref, wout_ref, g_ref, b_ref, o_ref):
    o_ref[...] = _ffn_block(x_ref[...], win_ref, wout_ref, g_ref, b_ref)


def _ffn_ln(x, w_in, w_out, g, b):
    t, d = x.shape
    tm = min(ROW_TILE, t)
    row = pl.BlockSpec((tm, d), lambda i: (i, 0))
    return pl.pallas_call(
        _ffn_ln_kernel,
        grid=(t // tm,),
        in_specs=[row, _resident(w_in.shape), _resident(w_out.shape),
                  _resident((1, d)), _resident((1, d))],
        out_specs=row,
        out_shape=jax.ShapeDtypeStruct((t, d), F32),
        compiler_params=_params("parallel"),
        name="ffn_ln",
    )(x, w_in, w_out, g.reshape(1, d), b.reshape(1, d))


def _mem_kv_kernel(mem_ref, w_ref, k_ref, v_ref):
    kv = _dot(mem_ref[...].astype(BF16), w_ref[...])
    k_ref[...] = kv[:, :MEM_WIDTH].astype(BF16)
    v_ref[...] = kv[:, MEM_WIDTH:].astype(BF16)


def _mem_kv(mem, w_kv):
    b, m, d = mem.shape
    out = pl.BlockSpec((None, m, MEM_WIDTH), lambda i: (i, 0, 0))
    return pl.pallas_call(
        _mem_kv_kernel,
        grid=(b,),
        in_specs=[pl.BlockSpec((None, m, d), lambda i: (i, 0, 0)), _resident(w_kv.shape)],
        out_specs=[out, out],
        out_shape=[jax.ShapeDtypeStruct((b, m, MEM_WIDTH), BF16)] * 2,
        compiler_params=_params("parallel"),
        name="mem_kv",
    )(mem, w_kv)


def _memory_attention(q_mem, km_ref, vm_ref):
    tm = q_mem.shape[0]
    low = _first_head((tm, LANES), 1)
    scale = 1.0 / math.sqrt(MEM_HEAD_DIM)
    outs = []
    for p in range(MEM_WIDTH // LANES):
        q2 = q_mem[:, p * LANES:(p + 1) * LANES]
        k2 = km_ref[:, p * LANES:(p + 1) * LANES]
        v2 = vm_ref[:, p * LANES:(p + 1) * LANES]
        pair = None
        for own in (low, jnp.logical_not(low)):
            qh = jnp.where(own, q2, 0.0).astype(BF16)
            s = _dot_nt(qh, k2) * scale
            e = jnp.exp(s - jnp.max(s, axis=-1, keepdims=True))
            prob = e / jnp.sum(e, axis=-1, keepdims=True)
            o = _dot(prob.astype(BF16), v2)
            pair = o if pair is None else jnp.where(low, pair, o)
        outs.append(pair)
    return jnp.concatenate(outs, axis=-1)


def _sb_inproj_kernel(x_ref, wqt_ref, wk_ref, wvt_ref, wqm_ref, km_ref, vm_ref,
                      qt_ref, k_ref, vt_ref, omem_ref):
    xb = x_ref[...].astype(BF16)
    qt_ref[...] = (_dot_nt(wqt_ref[...], xb) * (1.0 / math.sqrt(HEAD_DIM))).astype(BF16)
    k_ref[...] = _dot(xb, wk_ref[...]).astype(BF16)
    vt_ref[...] = _dot_nt(wvt_ref[...], xb).astype(BF16)
    omem_ref[...] = _memory_attention(_dot(xb, wqm_ref[...]), km_ref, vm_ref).astype(BF16)


def _sb_inproj(x, w_in, km, vm):
    b, s, d = x.shape
    m = km.shape[1]
    tm = min(PROJ_ROW_TILE, s)
    w = SB_WIDTH
    weights = [w_in[:, :w].T, w_in[:, w:2 * w], w_in[:, 2 * w:3 * w].T, w_in[:, 3 * w:]]
    mem = pl.BlockSpec((None, m, MEM_WIDTH), lambda bi, i: (bi, 0, 0))
    rows = lambda width: pl.BlockSpec((None, tm, width), lambda bi, i: (bi, i, 0))
    cols = pl.BlockSpec((None, w, tm), lambda bi, i: (bi, 0, i))
    return pl.pallas_call(
        _sb_inproj_kernel,
        grid=(b, s // tm),
        in_specs=[rows(d)] + [_resident(a.shape) for a in weights] + [mem, mem],
        out_specs=[cols, rows(w), cols, rows(MEM_WIDTH)],
        out_shape=[jax.ShapeDtypeStruct((b, w, s), BF16), jax.ShapeDtypeStruct((b, s, w), BF16),
                   jax.ShapeDtypeStruct((b, w, s), BF16), jax.ShapeDtypeStruct((b, s, MEM_WIDTH), BF16)],
        compiler_params=_params("parallel", "parallel"),
        name="sb_inproj",
    )(x, *weights, km, vm)


def _split3(x):
    hi = x.astype(BF16)
    r = x - hi.astype(F32)
    mid = r.astype(BF16)
    lo = (r - mid.astype(F32)).astype(BF16)
    return hi, mid, lo


def _later3(n):
    row = lax.broadcasted_iota(jnp.int32, (n, n), 0)
    col = lax.broadcasted_iota(jnp.int32, (n, n), 1)
    later = jnp.where(col > row, 1.0, 0.0).astype(BF16)
    return jnp.concatenate([later, later, later], axis=1)


def _sb_logs(z, visible):
    decay = jnp.log(1.0 + jnp.exp2(jnp.abs(z) * -math.log2(math.e))) + jnp.maximum(z, 0.0)
    return z - decay, jnp.where(visible, decay, 0.0)


def _sb_behind(later3, decay):
    return _dot(later3, jnp.concatenate(_split3(decay), axis=0))


def _sb_chain(qt, k_ref, vt_ref, start, n_keys, first_query, tail_in):
    key = lax.broadcasted_iota(jnp.int32, (n_keys, SB_SUB), 0)
    qry = lax.broadcasted_iota(jnp.int32, (n_keys, SB_SUB), 1)
    z = _dot(k_ref[pl.ds(start, n_keys), :], qt)
    visible = (key - qry) < (first_query - start)
    log_beta, decay = _sb_logs(z, visible)
    behind = _sb_behind(_later3(n_keys), decay)
    w = jnp.where(visible, jnp.exp(log_beta - behind + tail_in), 0.0)
    out_t = _dot(vt_ref[:, pl.ds(start, n_keys)], w.astype(BF16))
    return out_t, tail_in - jnp.sum(decay, axis=0, keepdims=True)


def _sb_attn_kernel(qt_ref, k_ref, vt_ref, o_ref, acc_ref, tail_ref):
    tq = qt_ref.shape[1]
    n_sub = tq // SB_SUB
    q0 = pl.program_id(2) * tq
    first = _first_head((LANES, SB_SUB), 0)
    owners = (first, jnp.logical_not(first))
    subs = [slice(u * SB_SUB, (u + 1) * SB_SUB) for u in range(n_sub)]

    def head_queries(u, h):
        return jnp.where(owners[h], qt_ref[:, subs[u]].astype(F32), 0.0).astype(BF16)

    starts = [pl.multiple_of(jnp.maximum(q0 + (u - 1) * SB_SUB, 0), SB_SUB) for u in range(n_sub)]
    later3 = _later3(SB_WINDOW)
    key = lax.broadcasted_iota(jnp.int32, (SB_WINDOW, 2 * SB_SUB), 0)
    qry = lax.broadcasted_iota(jnp.int32, (SB_WINDOW, 2 * SB_SUB), 1) & (SB_SUB - 1)
    ahead = key - qry
    zs = [_dot(k_ref[pl.ds(starts[u], SB_WINDOW), :],
               jnp.concatenate([head_queries(u, 0), head_queries(u, 1)], axis=1)) for u in range(n_sub)]
    stage = []
    for u in range(n_sub):
        visible = ahead < (q0 + u * SB_SUB - starts[u])
        log_beta, decay = _sb_logs(zs[u], visible)
        stage.append((visible, log_beta, decay, _sb_behind(later3, decay)))
    tail_max = {}
    pending = jnp.float32(-jnp.inf)
    for u in range(n_sub):
        visible, log_beta, decay, behind = stage[u]
        w = jnp.where(visible, jnp.exp(log_beta - behind), 0.0)
        out_t = _dot(vt_ref[:, pl.ds(starts[u], SB_WINDOW)], w.astype(BF16))
        tail = -jnp.sum(decay, axis=0, keepdims=True)
        for h in range(2):
            acc_ref[h, :, subs[u]] = out_t[:, h * SB_SUB:(h + 1) * SB_SUB]
            tail_ref[h, :, subs[u]] = tail[:, h * SB_SUB:(h + 1) * SB_SUB]
            tail_max[u, h] = jnp.max(tail[:, h * SB_SUB:(h + 1) * SB_SUB])
            pending = jnp.maximum(pending, jnp.where(starts[u] > 0, tail_max[u, h], -jnp.inf))

    @pl.when(pending > SB_LOG_WEIGHT_FLOOR)
    def _walk_further_back():
        for u in range(n_sub):
            for h in range(2):
                qt = head_queries(u, h)

                def cond(state):
                    start, tail_max = state
                    return jnp.logical_and(start > 0, tail_max > SB_LOG_WEIGHT_FLOOR)

                def body(state, qt=qt, u=u, h=h):
                    start = pl.multiple_of(state[0] - SB_SUB, SB_SUB)
                    out_t, tail = _sb_chain(qt, k_ref, vt_ref, start, SB_SUB,
                                            q0 + u * SB_SUB, tail_ref[h, :, subs[u]])
                    acc_ref[h, :, subs[u]] += out_t
                    tail_ref[h, :, subs[u]] = tail
                    return start, jnp.max(tail)

                lax.while_loop(cond, body, (starts[u], tail_max[u, h]))

    out_t = jnp.where(_first_head((LANES, tq), 0), acc_ref[0], acc_ref[1])
    o_ref[...] = out_t.T.astype(o_ref.dtype)


def _sb_attention(qt, k, vt):
    b, s, _ = k.shape
    tq = min(SB_QUERY_TILE, s)
    pairs = SB_WIDTH // LANES
    return pl.pallas_call(
        _sb_attn_kernel,
        grid=(b, pairs, s // tq),
        in_specs=[pl.BlockSpec((None, LANES, tq), lambda bi, p, i: (bi, p, i)),
                  pl.BlockSpec((None, s, LANES), lambda bi, p, i: (bi, 0, p)),
                  pl.BlockSpec((None, LANES, s), lambda bi, p, i: (bi, p, 0))],
        out_specs=pl.BlockSpec((None, tq, LANES), lambda bi, p, i: (bi, i, p)),
        out_shape=jax.ShapeDtypeStruct((b, s, SB_WIDTH), BF16),
        scratch_shapes=[pltpu.VMEM((2, LANES, tq), F32), pltpu.VMEM((2, 1, tq), F32)],
        compiler_params=_params("parallel", "parallel", "arbitrary"),
        name="sb_attention",
    )(qt, k, vt)


def _mla_inproj_kernel(x_ref, wcq_ref, wckv_ref, wkr_ref, wkrr_ref, wqm_ref, gq_ref, gkv_ref,
                       wqt_ref, wk_ref, wvt_ref, cos_ref, sin_ref, cost_ref, sint_ref,
                       km_ref, vm_ref, qt_ref, k_ref, vt_ref, omem_ref):
    xb = x_ref[...].astype(BF16)
    cos, sin = cos_ref[...], sin_ref[...]
    cos_t, sin_t = cost_ref[...], sint_ref[...]
    c_q = _rms_norm(_dot(xb, wcq_ref[...]), gq_ref[...]).astype(BF16)
    c_kv = _rms_norm(_dot(xb, wckv_ref[...]), gkv_ref[...]).astype(BF16)
    k_rope = _dot(xb, wkr_ref[...]) * cos + _dot(xb, wkrr_ref[...]) * sin
    k_rope2 = jnp.concatenate([k_rope, k_rope], axis=-1)
    r0, r1, r2 = MLA_NOPE_DIM, MLA_NOPE_DIM + MLA_ROPE_DIM // 2, MLA_NOPE_DIM + MLA_ROPE_DIM
    for p in range(N_MLA_HEADS // 2):
        lo, hi = 2 * p * LANES, 2 * (p + 1) * LANES
        q_pair = _dot_nt(wqt_ref[lo:hi, :], c_q)
        for h in range(2):
            q_t = q_pair[h * LANES:(h + 1) * LANES]
            rotated = jnp.concatenate([q_t[:r0], -q_t[r1:r2], q_t[r0:r1], q_t[r2:]], axis=0)
            q_t = (q_t * cos_t + rotated * sin_t) * MLA_LOG2_SCALE
            qt_ref[lo + h * LANES:lo + (h + 1) * LANES, :] = q_t.astype(BF16)
        k_ref[:, lo:hi] = (_dot(c_kv, wk_ref[:, lo:hi]) + k_rope2).astype(BF16)
    v_t = _dot_nt(wvt_ref[...], c_kv).astype(BF16)
    ones = jnp.ones((MLA_VT_ROWS - MLA_V_DIM, v_t.shape[1]), BF16)
    for h in range(N_MLA_HEADS):
        vt_ref[h * MLA_VT_ROWS:h * MLA_VT_ROWS + MLA_V_DIM, :] = v_t[h * MLA_V_DIM:(h + 1) * MLA_V_DIM, :]
        vt_ref[h * MLA_VT_ROWS + MLA_V_DIM:(h + 1) * MLA_VT_ROWS, :] = ones
    omem_ref[...] = _memory_attention(_dot(xb, wqm_ref[...]), km_ref, vm_ref).astype(BF16)


def _mla_inproj(x, w, cos, sin, km, vm):
    b, s, d = x.shape
    m = km.shape[1]
    tm = min(PROJ_ROW_TILE, s)
    hw = N_MLA_HEADS * LANES
    mem = pl.BlockSpec((None, m, MEM_WIDTH), lambda bi, i: (bi, 0, 0))
    table = pl.BlockSpec((tm, LANES), lambda bi, i: (i, 0))
    table_t = pl.BlockSpec((LANES, tm), lambda bi, i: (0, i))
    rows = lambda width: pl.BlockSpec((None, tm, width), lambda bi, i: (bi, i, 0))
    cols = lambda width: pl.BlockSpec((None, width, tm), lambda bi, i: (bi, 0, i))
    weights = [w["cq"], w["ckv"], w["kr"], w["kr_rot"], w["qmem"], w["gq"], w["gkv"],
               w["q"].T, w["k"], w["v"].T]
    return pl.pallas_call(
        _mla_inproj_kernel,
        grid=(b, s // tm),
        in_specs=([rows(d)] + [_resident(a.shape) for a in weights]
                  + [table, table, table_t, table_t, mem, mem]),
        out_specs=[cols(hw), rows(hw), cols(N_MLA_HEADS * MLA_VT_ROWS), rows(MEM_WIDTH)],
        out_shape=[jax.ShapeDtypeStruct((b, hw, s), BF16), jax.ShapeDtypeStruct((b, s, hw), BF16),
                   jax.ShapeDtypeStruct((b, N_MLA_HEADS * MLA_VT_ROWS, s), BF16),
                   jax.ShapeDtypeStruct((b, s, MEM_WIDTH), BF16)],
        compiler_params=_params("parallel", "parallel"),
        name="mla_inproj",
    )(x, *weights, cos, sin, cos.T, sin.T, km, vm)


def _mla_attn_kernel(qt_ref, k_ref, vt_ref, o_ref, s0_ref, s1_ref, b0_ref, b1_ref, m_ref, acc_ref):
    t = qt_ref.shape[1]
    i = pl.program_id(2)
    m_ref[...] = jnp.full_like(m_ref, -jnp.inf)
    acc_ref[...] = jnp.zeros_like(acc_ref)

    def scores(j, s_ref, b_ref):
        start = pl.multiple_of(j * t, t)
        for h in range(2):
            tile = slice(h * LANES, (h + 1) * LANES)
            s_t = _dot(k_ref[pl.ds(start, t), tile], qt_ref[tile, :])
            s_ref[h] = s_t
            b_ref[h] = jnp.max(s_t, axis=0, keepdims=True)

    def update(j, s_ref, b_ref, masked):
        start = pl.multiple_of(j * t, t)
        for h in range(2):
            s_t = s_ref[h]
            if masked:
                key = lax.broadcasted_iota(jnp.int32, (t, t), 0)
                qry = lax.broadcasted_iota(jnp.int32, (t, t), 1)
                s_t = jnp.where(key <= qry, s_t, -jnp.inf)
                block_max = jnp.max(s_t, axis=0, keepdims=True)
            else:
                block_max = b_ref[h]
            m_old = m_ref[h]
            m_new = jnp.maximum(m_old, block_max)
            alpha = jnp.exp2(m_old - m_new)
            prob_t = jnp.exp2(s_t - m_new)
            v_t = vt_ref[h * MLA_VT_ROWS:(h + 1) * MLA_VT_ROWS, pl.ds(start, t)]
            acc_ref[h] = alpha * acc_ref[h] + _dot(v_t, prob_t.astype(BF16))
            m_ref[h] = m_new

    def two_full_steps(j):
        scores(j + 1, s1_ref, b1_ref)
        update(j, s0_ref, b0_ref, masked=False)
        scores(j + 2, s0_ref, b0_ref)
        update(j + 1, s1_ref, b1_ref, masked=False)

    def eight_full_steps(jj, carry):
        for step in range(0, MLA_UNROLL, 2):
            two_full_steps(MLA_UNROLL * jj + step)
        return carry

    scores(0, s0_ref, b0_ref)
    lax.fori_loop(0, i // MLA_UNROLL, eight_full_steps, 0)
    span = MLA_UNROLL // 2
    while span >= 2:
        @pl.when(i % (2 * span) >= span)
        def _more_full_steps(span=span):
            for step in range(0, span, 2):
                two_full_steps((2 * span) * (i // (2 * span)) + step)
        span //= 2

    @pl.when(i % 2 == 0)
    def _diagonal_in_s0():
        update(i, s0_ref, b0_ref, masked=True)

    @pl.when(i % 2 == 1)
    def _last_full_then_diagonal():
        scores(i, s1_ref, b1_ref)
        update(i - 1, s0_ref, b0_ref, masked=False)
        update(i, s1_ref, b1_ref, masked=True)

    heads = [acc_ref[h, :MLA_V_DIM, :] / acc_ref[h, MLA_V_DIM:MLA_V_DIM + 1, :] for h in range(2)]
    o_ref[...] = jnp.concatenate(heads, axis=0).T.astype(o_ref.dtype)


def _mla_attention(qt, k, vt):
    b, s, _ = k.shape
    t = min(MLA_TILE, s)
    return pl.pallas_call(
        _mla_attn_kernel,
        grid=(b, N_MLA_HEADS // 2, s // t),
        in_specs=[pl.BlockSpec((None, 2 * LANES, t), lambda bi, p, i: (bi, p, i)),
                  pl.BlockSpec((None, s, 2 * LANES), lambda bi, p, i: (bi, 0, p)),
                  pl.BlockSpec((None, 2 * MLA_VT_ROWS, s), lambda bi, p, i: (bi, p, 0))],
        out_specs=pl.BlockSpec((None, t, LANES), lambda bi, p, i: (bi, i, p)),
        out_shape=jax.ShapeDtypeStruct((b, s, MLA_WIDTH), BF16),
        scratch_shapes=[pltpu.VMEM((2, t, t), F32), pltpu.VMEM((2, t, t), F32),
                        pltpu.VMEM((2, 1, t), F32), pltpu.VMEM((2, 1, t), F32),
                        pltpu.VMEM((2, 1, t), F32), pltpu.VMEM((2, MLA_VT_ROWS, t), F32)],
        compiler_params=_params("parallel", "parallel", "arbitrary"),
        name="mla_attention",
    )(qt, k, vt)


def _outproj_ffn_kernel(x_ref, a_ref, m_ref, wo_ref, gm_ref, bm_ref, win_ref, wout_ref, g_ref, b_ref, o_ref):
    width = a_ref.shape[1]
    mixed = _dot(a_ref[...], wo_ref[:width, :]) + _dot(m_ref[...], wo_ref[width:, :])
    x = _layer_norm(DEEPNORM_ALPHA * x_ref[...] + mixed, gm_ref[...], bm_ref[...])
    o_ref[...] = _ffn_block(x, win_ref, wout_ref, g_ref, b_ref)


def _outproj_ffn(x, attn, omem, w_out, g_mix, b_mix, w_in, w_ffn_out, g, b):
    t, d = x.shape
    tm = min(ROW_TILE, t)
    rows = lambda width: pl.BlockSpec((tm, width), lambda i: (i, 0))
    vec = lambda a: a.reshape(1, d)
    return pl.pallas_call(
        _outproj_ffn_kernel,
        grid=(t // tm,),
        in_specs=[rows(d), rows(attn.shape[1]), rows(omem.shape[1]), _resident(w_out.shape),
                  _resident((1, d)), _resident((1, d)), _resident(w_in.shape),
                  _resident(w_ffn_out.shape), _resident((1, d)), _resident((1, d))],
        out_specs=rows(d),
        out_shape=jax.ShapeDtypeStruct((t, d), F32),
        compiler_params=_params("parallel"),
        name="outproj_ffn",
    )(x, attn, omem, w_out, vec(g_mix), vec(b_mix), w_in, w_ffn_out, vec(g), vec(b))


def _rotate_half_columns(w):
    half = w.shape[1] // 2
    return jnp.concatenate([-w[:, half:], w[:, :half]], axis=1)


def _head_tiles(cols_per_head, offset):
    k, h, c = cols_per_head.shape
    tiles = jnp.zeros((k, h, LANES), cols_per_head.dtype).at[:, :, offset:offset + c].set(cols_per_head)
    return tiles.reshape(k, h * LANES)


def _mla_weights(w_in, q_norm_g, w_uq, kv_norm_g, w_ukv):
    q_dim = MLA_NOPE_DIM + MLA_ROPE_DIM
    kr0 = MLA_Q_RANK + MLA_KV_RANK
    w_kr = w_in[:, kr0:kr0 + MLA_ROPE_DIM]
    uq = w_uq.reshape(MLA_Q_RANK, N_MLA_HEADS, q_dim)
    ukv = w_ukv.reshape(MLA_KV_RANK, N_MLA_HEADS, MLA_NOPE_DIM + MLA_V_DIM)
    as_tile = lambda w, off: _head_tiles(w[:, None, :], off)
    w = {
        "cq": w_in[:, :MLA_Q_RANK],
        "ckv": w_in[:, MLA_Q_RANK:kr0],
        "kr": as_tile(w_kr, MLA_NOPE_DIM),
        "kr_rot": as_tile(_rotate_half_columns(w_kr), MLA_NOPE_DIM),
        "qmem": w_in[:, kr0 + MLA_ROPE_DIM:],
        "q": _head_tiles(uq, 0),
        "k": _head_tiles(ukv[:, :, :MLA_NOPE_DIM], 0),
        "v": ukv[:, :, MLA_NOPE_DIM:].reshape(MLA_KV_RANK, MLA_WIDTH),
    }
    w = {name: a.astype(BF16) for name, a in w.items()}
    w["gq"] = q_norm_g.reshape(1, MLA_Q_RANK)
    w["gkv"] = kv_norm_g.reshape(1, MLA_KV_RANK)
    return w


def _rope_tables(s):
    freqs = ROPE_BASE ** (-jnp.arange(0, MLA_ROPE_DIM, 2, dtype=F32) / MLA_ROPE_DIM)
    ang = jnp.arange(s).astype(F32)[:, None] * freqs[None, :]
    pad = jnp.zeros((s, LANES - MLA_NOPE_DIM - MLA_ROPE_DIM), F32)
    cos = jnp.concatenate([jnp.ones((s, MLA_NOPE_DIM), F32), jnp.cos(ang), jnp.cos(ang), pad], axis=1)
    sin = jnp.concatenate([jnp.zeros((s, MLA_NOPE_DIM), F32), jnp.sin(ang), jnp.sin(ang), pad], axis=1)
    return cos, sin


def kernel(x, mem, ln_ffn1_g, ln_ffn1_b, ln_mix_g, ln_mix_b, ln_ffn2_g, ln_ffn2_b, ffn1_w_in, ffn1_w_out, ffn2_w_in, ffn2_w_out, sb_w_in, mla_w_in, mla_q_norm_g, mla_w_uq, mla_kv_norm_g, mla_w_ukv, mem_w_kv, w_out):
    b, s, d = x.shape
    flat = lambda a: a.reshape(b * s, a.shape[-1])
    cos, sin = _rope_tables(s)
    for i in range(DEPTH):
        x = _ffn_ln(flat(x), ffn1_w_in[i].astype(BF16), ffn1_w_out[i].astype(BF16),
                    ln_ffn1_g[i], ln_ffn1_b[i]).reshape(b, s, d)
        km, vm = _mem_kv(mem, mem_w_kv[i].astype(BF16))
        j = i // 2
        if i % 2 == 0:
            qt, k, vt, omem = _sb_inproj(x, sb_w_in[j].astype(BF16), km, vm)
            attn = _sb_attention(qt, k, vt)
        else:
            w = _mla_weights(mla_w_in[j], mla_q_norm_g[j], mla_w_uq[j], mla_kv_norm_g[j], mla_w_ukv[j])
            qt, k, vt, omem = _mla_inproj(x, w, cos, sin, km, vm)
            attn = _mla_attention(qt, k, vt)
        x = _outproj_ffn(flat(x), flat(attn), flat(omem), w_out[i].astype(BF16), ln_mix_g[i], ln_mix_b[i],
                         ffn2_w_in[i].astype(BF16), ffn2_w_out[i].astype(BF16),
                         ln_ffn2_g[i], ln_ffn2_b[i]).reshape(b, s, d)
    return x
```

```python
import math

import jax
import jax.numpy as jnp
from jax import lax
from jax.experimental import pallas as pl
from jax.experimental.pallas import tpu as pltpu

D_MODEL = 1024
DEPTH = 2
HEAD_DIM = 64
N_SB_HEADS = 12
N_MLA_HEADS = 12
MLA_NOPE_DIM = 64
MLA_ROPE_DIM = 32
MLA_V_DIM = 64
MLA_Q_RANK = 384
MLA_KV_RANK = 256
ROPE_BASE = 10000.0
N_MEM_HEADS = 4
MEM_HEAD_DIM = 64
MEM_WIDTH = N_MEM_HEADS * MEM_HEAD_DIM
SB_WIDTH = N_SB_HEADS * HEAD_DIM
MLA_WIDTH = N_MLA_HEADS * MLA_V_DIM
D_FF = 2816
LN_EPS = 1e-5
RMS_EPS = 1e-6
DEEPNORM_ALPHA = (2 * DEPTH) ** 0.25

LANES = 128
VMEM_LIMIT_BYTES = 56 * 1024 * 1024

ROW_TILE = 512
PROJ_ROW_TILE = 1024
MXU_TILE = 256
FFN_SPLIT = (0, 6 * MXU_TILE, D_FF)
FFN_CAST_STEPS = D_FF // MXU_TILE
BF16_ROWS = 16
MLA_VT_ROWS = MLA_V_DIM + BF16_ROWS
SB_QUERY_TILE = 2048
SB_SUB = LANES
SB_WINDOW = 2 * SB_SUB
MLA_TILE = 512
MLA_UNROLL = 8
SB_LOG_WEIGHT_FLOOR = -88.0
MLA_LOG2_SCALE = math.log2(math.e) / math.sqrt(MLA_NOPE_DIM + MLA_ROPE_DIM)

BF16 = jnp.bfloat16
F32 = jnp.float32


def _dot(a, b):
    return jnp.dot(a, b, preferred_element_type=F32)


def _dot_nt(a, b):
    return lax.dot_general(a, b, (((1,), (1,)), ((), ())), preferred_element_type=F32)


def _layer_norm(y, g, b):
    mu = jnp.mean(y, axis=-1, keepdims=True)
    d = y - mu
    var = jnp.mean(d * d, axis=-1, keepdims=True)
    return d * lax.rsqrt(var + LN_EPS) * g + b


def _rms_norm(x, g):
    return x * lax.rsqrt(jnp.mean(x * x, axis=-1, keepdims=True) + RMS_EPS) * g


def _first_head(shape, axis):
    return lax.broadcasted_iota(jnp.int32, shape, axis) < HEAD_DIM


def _params(*semantics):
    return pltpu.CompilerParams(dimension_semantics=semantics, vmem_limit_bytes=VMEM_LIMIT_BYTES)


def _resident(shape):
    nd = len(shape)
    return pl.BlockSpec(shape, lambda *_: (0,) * nd, pipeline_mode=pl.Buffered(1))


def _ffn_block(x, win_ref, wout_ref, g_ref, b_ref):
    xb = x.astype(BF16)
    acc = None
    for lo, hi in zip(FFN_SPLIT[:-1], FFN_SPLIT[1:]):
        gate = _dot(xb, win_ref[:, lo:hi])
        up = _dot(xb, win_ref[:, D_FF + lo:D_FF + hi])
        act = (gate * jax.nn.sigmoid(gate) * up).astype(BF16)
        part = _dot(act, wout_ref[lo:hi, :])
        acc = part if acc is None else acc + part
    return _layer_norm(DEEPNORM_ALPHA * x + 0.5 * acc, g_ref[...], b_ref[...])


def _cast_ffn_slab(step, win32_ref, wout32_ref, win_ref, wout_ref):
    cols, rows = win32_ref.shape[1], wout32_ref.shape[0]
    win_ref[:, pl.ds(pl.multiple_of(step * cols, cols), cols)] = win32_ref[...].astype(BF16)
    wout_ref[pl.ds(pl.multiple_of(step * rows, rows), rows), :] = wout32_ref[...].astype(BF16)


def _ffn_weight_specs(layer, d):
    last = FFN_CAST_STEPS - 1
    return [pl.BlockSpec((None, d, 2 * D_FF // FFN_CAST_STEPS), lambda i: (layer, 0, jnp.minimum(i, last))),
            pl.BlockSpec((None, D_FF // FFN_CAST_STEPS, d), lambda i: (layer, jnp.minimum(i, last), 0))]


def _ffn_weight_scratch(d):
    return [pltpu.VMEM((d, 2 * D_FF), BF16), pltpu.VMEM((D_FF, d), BF16)]


def _row_tile_spec(tm, width):
    return pl.BlockSpec((tm, width), lambda i: (jnp.maximum(i - FFN_CAST_STEPS, 0), 0))


def _ffn_ln_kernel(x_ref, win32_ref, wout32_ref, g_ref, b_ref, o_ref, win_ref, wout_ref):
    step = pl.program_id(0)

    @pl.when(step < FFN_CAST_STEPS)
    def _cast():
        _cast_ffn_slab(step, win32_ref, wout32_ref, win_ref, wout_ref)

    @pl.when(step >= FFN_CAST_STEPS)
    def _compute():
        o_ref[...] = _ffn_block(x_ref[...], win_ref, wout_ref, g_ref, b_ref)


def _ffn_ln(x, w_in, w_out, layer, g, b):
    t, d = x.shape
    tm = min(ROW_TILE, t)
    return pl.pallas_call(
        _ffn_ln_kernel,
        grid=(FFN_CAST_STEPS + t // tm,),
        in_specs=[_row_tile_spec(tm, d)] + _ffn_weight_specs(layer, d) + [_resident((1, d)), _resident((1, d))],
        out_specs=_row_tile_spec(tm, d),
        out_shape=jax.ShapeDtypeStruct((t, d), F32),
        scratch_shapes=_ffn_weight_scratch(d),
        compiler_params=_params("arbitrary"),
        name="ffn_ln",
    )(x, w_in, w_out, g.reshape(1, d), b.reshape(1, d))


def _mem_kv_kernel(mem_ref, w_ref, k_ref, v_ref):
    kv = _dot(mem_ref[...].astype(BF16), w_ref[...])
    k_ref[...] = kv[:, :MEM_WIDTH].astype(BF16)
    v_ref[...] = kv[:, MEM_WIDTH:].astype(BF16)


def _mem_kv(mem, w_kv):
    b, m, d = mem.shape
    out = pl.BlockSpec((None, m, MEM_WIDTH), lambda i: (i, 0, 0))
    return pl.pallas_call(
        _mem_kv_kernel,
        grid=(b,),
        in_specs=[pl.BlockSpec((None, m, d), lambda i: (i, 0, 0)), _resident(w_kv.shape)],
        out_specs=[out, out],
        out_shape=[jax.ShapeDtypeStruct((b, m, MEM_WIDTH), BF16)] * 2,
        compiler_params=_params("parallel"),
        name="mem_kv",
    )(mem, w_kv)


def _memory_attention(q_mem, km_ref, vm_ref):
    tm = q_mem.shape[0]
    low = _first_head((tm, LANES), 1)
    scale = 1.0 / math.sqrt(MEM_HEAD_DIM)
    outs = []
    for p in range(MEM_WIDTH // LANES):
        q2 = q_mem[:, p * LANES:(p + 1) * LANES]
        k2 = km_ref[:, p * LANES:(p + 1) * LANES]
        v2 = vm_ref[:, p * LANES:(p + 1) * LANES]
        pair = None
        for own in (low, jnp.logical_not(low)):
            qh = jnp.where(own, q2, 0.0).astype(BF16)
            s = _dot_nt(qh, k2) * scale
            e = jnp.exp(s - jnp.max(s, axis=-1, keepdims=True))
            prob = e / jnp.sum(e, axis=-1, keepdims=True)
            o = _dot(prob.astype(BF16), v2)
            pair = o if pair is None else jnp.where(low, pair, o)
        outs.append(pair)
    return jnp.concatenate(outs, axis=-1)


def _sb_inproj_kernel(x_ref, wqt_ref, wk_ref, wvt_ref, wqm_ref, km_ref, vm_ref,
                      qt_ref, k_ref, vt_ref, omem_ref):
    xb = x_ref[...].astype(BF16)
    qt_ref[...] = (_dot_nt(wqt_ref[...], xb) * (1.0 / math.sqrt(HEAD_DIM))).astype(BF16)
    k_ref[...] = _dot(xb, wk_ref[...]).astype(BF16)
    vt_ref[...] = _dot_nt(wvt_ref[...], xb).astype(BF16)
    omem_ref[...] = _memory_attention(_dot(xb, wqm_ref[...]), km_ref, vm_ref).astype(BF16)


def _sb_inproj(x, w_in, km, vm):
    b, s, d = x.shape
    m = km.shape[1]
    tm = min(PROJ_ROW_TILE, s)
    w = SB_WIDTH
    weights = [w_in[:, :w].T, w_in[:, w:2 * w], w_in[:, 2 * w:3 * w].T, w_in[:, 3 * w:]]
    mem = pl.BlockSpec((None, m, MEM_WIDTH), lambda bi, i: (bi, 0, 0))
    rows = lambda width: pl.BlockSpec((None, tm, width), lambda bi, i: (bi, i, 0))
    cols = pl.BlockSpec((None, w, tm), lambda bi, i: (bi, 0, i))
    return pl.pallas_call(
        _sb_inproj_kernel,
        grid=(b, s // tm),
        in_specs=[rows(d)] + [_resident(a.shape) for a in weights] + [mem, mem],
        out_specs=[cols, rows(w), cols, rows(MEM_WIDTH)],
        out_shape=[jax.ShapeDtypeStruct((b, w, s), BF16), jax.ShapeDtypeStruct((b, s, w), BF16),
                   jax.ShapeDtypeStruct((b, w, s), BF16), jax.ShapeDtypeStruct((b, s, MEM_WIDTH), BF16)],
        compiler_params=_params("parallel", "parallel"),
        name="sb_inproj",
    )(x, *weights, km, vm)


def _split3(x):
    hi = x.astype(BF16)
    r = x - hi.astype(F32)
    mid = r.astype(BF16)
    lo = (r - mid.astype(F32)).astype(BF16)
    return hi, mid, lo


def _later3(n):
    row = lax.broadcasted_iota(jnp.int32, (n, n), 0)
    col = lax.broadcasted_iota(jnp.int32, (n, n), 1)
    later = jnp.where(col > row, 1.0, 0.0).astype(BF16)
    return jnp.concatenate([later, later, later], axis=1)


def _sb_logs(z, visible):
    decay = jnp.log(1.0 + jnp.exp2(jnp.abs(z) * -math.log2(math.e))) + jnp.maximum(z, 0.0)
    return z - decay, jnp.where(visible, decay, 0.0)


def _sb_behind(later3, decay):
    return _dot(later3, jnp.concatenate(_split3(decay), axis=0))


def _sb_chain(qt, k_ref, vt_ref, start, n_keys, first_query, tail_in):
    key = lax.broadcasted_iota(jnp.int32, (n_keys, SB_SUB), 0)
    qry = lax.broadcasted_iota(jnp.int32, (n_keys, SB_SUB), 1)
    z = _dot(k_ref[pl.ds(start, n_keys), :], qt)
    visible = (key - qry) < (first_query - start)
    log_beta, decay = _sb_logs(z, visible)
    behind = _sb_behind(_later3(n_keys), decay)
    w = jnp.where(visible, jnp.exp(log_beta - behind + tail_in), 0.0)
    out_t = _dot(vt_ref[:, pl.ds(start, n_keys)], w.astype(BF16))
    return out_t, tail_in - jnp.sum(decay, axis=0, keepdims=True)


def _sb_attn_kernel(qt_ref, k_ref, vt_ref, o_ref, acc_ref, tail_ref):
    tq = qt_ref.shape[1]
    n_sub = tq // SB_SUB
    q0 = pl.program_id(2) * tq
    first = _first_head((LANES, SB_SUB), 0)
    owners = (first, jnp.logical_not(first))
    subs = [slice(u * SB_SUB, (u + 1) * SB_SUB) for u in range(n_sub)]

    def head_queries(u, h):
        return jnp.where(owners[h], qt_ref[:, subs[u]].astype(F32), 0.0).astype(BF16)

    starts = [pl.multiple_of(jnp.maximum(q0 + (u - 1) * SB_SUB, 0), SB_SUB) for u in range(n_sub)]
    later3 = _later3(SB_WINDOW)
    key = lax.broadcasted_iota(jnp.int32, (SB_WINDOW, 2 * SB_SUB), 0)
    qry = lax.broadcasted_iota(jnp.int32, (SB_WINDOW, 2 * SB_SUB), 1) & (SB_SUB - 1)
    ahead = key - qry
    zs = [_dot(k_ref[pl.ds(starts[u], SB_WINDOW), :],
               jnp.concatenate([head_queries(u, 0), head_queries(u, 1)], axis=1)) for u in range(n_sub)]
    stage = []
    for u in range(n_sub):
        visible = ahead < (q0 + u * SB_SUB - starts[u])
        log_beta, decay = _sb_logs(zs[u], visible)
        stage.append((visible, log_beta, decay, _sb_behind(later3, decay)))
    tail_max = {}
    pending = jnp.float32(-jnp.inf)
    for u in range(n_sub):
        visible, log_beta, decay, behind = stage[u]
        w = jnp.where(visible, jnp.exp(log_beta - behind), 0.0)
        out_t = _dot(vt_ref[:, pl.ds(starts[u], SB_WINDOW)], w.astype(BF16))
        tail = -jnp.sum(decay, axis=0, keepdims=True)
        for h in range(2):
            acc_ref[h, :, subs[u]] = out_t[:, h * SB_SUB:(h + 1) * SB_SUB]
            tail_ref[h, :, subs[u]] = tail[:, h * SB_SUB:(h + 1) * SB_SUB]
            tail_max[u, h] = jnp.max(tail[:, h * SB_SUB:(h + 1) * SB_SUB])
            pending = jnp.maximum(pending, jnp.where(starts[u] > 0, tail_max[u, h], -jnp.inf))

    @pl.when(pending > SB_LOG_WEIGHT_FLOOR)
    def _walk_further_back():
        for u in range(n_sub):
            for h in range(2):
                qt = head_queries(u, h)

                def cond(state):
                    start, tail_max = state
                    return jnp.logical_and(start > 0, tail_max > SB_LOG_WEIGHT_FLOOR)

                def body(state, qt=qt, u=u, h=h):
                    start = pl.multiple_of(state[0] - SB_SUB, SB_SUB)
                    out_t, tail = _sb_chain(qt, k_ref, vt_ref, start, SB_SUB,
                                            q0 + u * SB_SUB, tail_ref[h, :, subs[u]])
                    acc_ref[h, :, subs[u]] += out_t
                    tail_ref[h, :, subs[u]] = tail
                    return start, jnp.max(tail)

                lax.while_loop(cond, body, (starts[u], tail_max[u, h]))

    out_t = jnp.where(_first_head((LANES, tq), 0), acc_ref[0], acc_ref[1])
    o_ref[...] = out_t.T.astype(o_ref.dtype)


def _sb_attention(qt, k, vt):
    b, s, _ = k.shape
    tq = min(SB_QUERY_TILE, s)
    pairs = SB_WIDTH // LANES
    return pl.pallas_call(
        _sb_attn_kernel,
        grid=(b, pairs, s // tq),
        in_specs=[pl.BlockSpec((None, LANES, tq), lambda bi, p, i: (bi, p, i)),
                  pl.BlockSpec((None, s, LANES), lambda bi, p, i: (bi, 0, p)),
                  pl.BlockSpec((None, LANES, s), lambda bi, p, i: (bi, p, 0))],
        out_specs=pl.BlockSpec((None, tq, LANES), lambda bi, p, i: (bi, i, p)),
        out_shape=jax.ShapeDtypeStruct((b, s, SB_WIDTH), BF16),
        scratch_shapes=[pltpu.VMEM((2, LANES, tq), F32), pltpu.VMEM((2, 1, tq), F32)],
        compiler_params=_params("parallel", "parallel", "arbitrary"),
        name="sb_attention",
    )(qt, k, vt)


def _mla_inproj_kernel(x_ref, wcq_ref, wckv_ref, wkr_ref, wkrr_ref, wqm_ref, gq_ref, gkv_ref,
                       wqt_ref, wk_ref, wvt_ref, cos_ref, sin_ref, cost_ref, sint_ref,
                       km_ref, vm_ref, qt_ref, k_ref, vt_ref, omem_ref):
    xb = x_ref[...].astype(BF16)
    cos, sin = cos_ref[...], sin_ref[...]
    cos_t, sin_t = cost_ref[...], sint_ref[...]
    c_q = _rms_norm(_dot(xb, wcq_ref[...]), gq_ref[...]).astype(BF16)
    c_kv = _rms_norm(_dot(xb, wckv_ref[...]), gkv_ref[...]).astype(BF16)
    k_rope = _dot(xb, wkr_ref[...]) * cos + _dot(xb, wkrr_ref[...]) * sin
    k_rope2 = jnp.concatenate([k_rope, k_rope], axis=-1)
    r0, r1, r2 = MLA_NOPE_DIM, MLA_NOPE_DIM + MLA_ROPE_DIM // 2, MLA_NOPE_DIM + MLA_ROPE_DIM
    for p in range(N_MLA_HEADS // 2):
        lo, hi = 2 * p * LANES, 2 * (p + 1) * LANES
        q_pair = _dot_nt(wqt_ref[lo:hi, :], c_q)
        for h in range(2):
            q_t = q_pair[h * LANES:(h + 1) * LANES]
            rotated = jnp.concatenate([q_t[:r0], -q_t[r1:r2], q_t[r0:r1], q_t[r2:]], axis=0)
            q_t = (q_t * cos_t + rotated * sin_t) * MLA_LOG2_SCALE
            qt_ref[lo + h * LANES:lo + (h + 1) * LANES, :] = q_t.astype(BF16)
        k_ref[:, lo:hi] = (_dot(c_kv, wk_ref[:, lo:hi]) + k_rope2).astype(BF16)
    v_t = _dot_nt(wvt_ref[...], c_kv).astype(BF16)
    ones = jnp.ones((MLA_VT_ROWS - MLA_V_DIM, v_t.shape[1]), BF16)
    for h in range(N_MLA_HEADS):
        vt_ref[h * MLA_VT_ROWS:h * MLA_VT_ROWS + MLA_V_DIM, :] = v_t[h * MLA_V_DIM:(h + 1) * MLA_V_DIM, :]
        vt_ref[h * MLA_VT_ROWS + MLA_V_DIM:(h + 1) * MLA_VT_ROWS, :] = ones
    omem_ref[...] = _memory_attention(_dot(xb, wqm_ref[...]), km_ref, vm_ref).astype(BF16)


def _mla_inproj(x, w, cos, sin, km, vm):
    b, s, d = x.shape
    m = km.shape[1]
    tm = min(PROJ_ROW_TILE, s)
    hw = N_MLA_HEADS * LANES
    mem = pl.BlockSpec((None, m, MEM_WIDTH), lambda bi, i: (bi, 0, 0))
    table = pl.BlockSpec((tm, LANES), lambda bi, i: (i, 0))
    table_t = pl.BlockSpec((LANES, tm), lambda bi, i: (0, i))
    rows = lambda width: pl.BlockSpec((None, tm, width), lambda bi, i: (bi, i, 0))
    cols = lambda width: pl.BlockSpec((None, width, tm), lambda bi, i: (bi, 0, i))
    weights = [w["cq"], w["ckv"], w["kr"], w["kr_rot"], w["qmem"], w["gq"], w["gkv"],
               w["q"].T, w["k"], w["v"].T]
    return pl.pallas_call(
        _mla_inproj_kernel,
        grid=(b, s // tm),
        in_specs=([rows(d)] + [_resident(a.shape) for a in weights]
                  + [table, table, table_t, table_t, mem, mem]),
        out_specs=[cols(hw), rows(hw), cols(N_MLA_HEADS * MLA_VT_ROWS), rows(MEM_WIDTH)],
        out_shape=[jax.ShapeDtypeStruct((b, hw, s), BF16), jax.ShapeDtypeStruct((b, s, hw), BF16),
                   jax.ShapeDtypeStruct((b, N_MLA_HEADS * MLA_VT_ROWS, s), BF16),
                   jax.ShapeDtypeStruct((b, s, MEM_WIDTH), BF16)],
        compiler_params=_params("parallel", "parallel"),
        name="mla_inproj",
    )(x, *weights, cos, sin, cos.T, sin.T, km, vm)


def _mla_attn_kernel(qt_ref, k_ref, vt_ref, o_ref, s0_ref, s1_ref, b0_ref, b1_ref, m_ref, acc_ref):
    t = qt_ref.shape[1]
    i = pl.program_id(2)
    m_ref[...] = jnp.full_like(m_ref, -jnp.inf)
    acc_ref[...] = jnp.zeros_like(acc_ref)

    def scores(j, s_ref, b_ref):
        start = pl.multiple_of(j * t, t)
        for h in range(2):
            tile = slice(h * LANES, (h + 1) * LANES)
            s_t = _dot(k_ref[pl.ds(start, t), tile], qt_ref[tile, :])
            s_ref[h] = s_t
            b_ref[h] = jnp.max(s_t, axis=0, keepdims=True)

    def update(j, s_ref, b_ref, masked):
        start = pl.multiple_of(j * t, t)
        for h in range(2):
            s_t = s_ref[h]
            if masked:
                key = lax.broadcasted_iota(jnp.int32, (t, t), 0)
                qry = lax.broadcasted_iota(jnp.int32, (t, t), 1)
                s_t = jnp.where(key <= qry, s_t, -jnp.inf)
                block_max = jnp.max(s_t, axis=0, keepdims=True)
            else:
                block_max = b_ref[h]
            m_old = m_ref[h]
            m_new = jnp.maximum(m_old, block_max)
            alpha = jnp.exp2(m_old - m_new)
            prob_t = jnp.exp2(s_t - m_new)
            v_t = vt_ref[h * MLA_VT_ROWS:(h + 1) * MLA_VT_ROWS, pl.ds(start, t)]
            acc_ref[h] = alpha * acc_ref[h] + _dot(v_t, prob_t.astype(BF16))
            m_ref[h] = m_new

    def two_full_steps(j):
        scores(j + 1, s1_ref, b1_ref)
        update(j, s0_ref, b0_ref, masked=False)
        scores(j + 2, s0_ref, b0_ref)
        update(j + 1, s1_ref, b1_ref, masked=False)

    def eight_full_steps(jj, carry):
        for step in range(0, MLA_UNROLL, 2):
            two_full_steps(MLA_UNROLL * jj + step)
        return carry

    scores(0, s0_ref, b0_ref)
    lax.fori_loop(0, i // MLA_UNROLL, eight_full_steps, 0)
    span = MLA_UNROLL // 2
    while span >= 2:
        @pl.when(i % (2 * span) >= span)
        def _more_full_steps(span=span):
            for step in range(0, span, 2):
                two_full_steps((2 * span) * (i // (2 * span)) + step)
        span //= 2

    @pl.when(i % 2 == 0)
    def _diagonal_in_s0():
        update(i, s0_ref, b0_ref, masked=True)

    @pl.when(i % 2 == 1)
    def _last_full_then_diagonal():
        scores(i, s1_ref, b1_ref)
        update(i - 1, s0_ref, b0_ref, masked=False)
        update(i, s1_ref, b1_ref, masked=True)

    heads = [acc_ref[h, :MLA_V_DIM, :] / acc_ref[h, MLA_V_DIM:MLA_V_DIM + 1, :] for h in range(2)]
    o_ref[...] = jnp.concatenate(heads, axis=0).T.astype(o_ref.dtype)


def _mla_attention(qt, k, vt):
    b, s, _ = k.shape
    t = min(MLA_TILE, s)
    return pl.pallas_call(
        _mla_attn_kernel,
        grid=(b, N_MLA_HEADS // 2, s // t),
        in_specs=[pl.BlockSpec((None, 2 * LANES, t), lambda bi, p, i: (bi, p, i)),
                  pl.BlockSpec((None, s, 2 * LANES), lambda bi, p, i: (bi, 0, p)),
                  pl.BlockSpec((None, 2 * MLA_VT_ROWS, s), lambda bi, p, i: (bi, p, 0))],
        out_specs=pl.BlockSpec((None, t, LANES), lambda bi, p, i: (bi, i, p)),
        out_shape=jax.ShapeDtypeStruct((b, s, MLA_WIDTH), BF16),
        scratch_shapes=[pltpu.VMEM((2, t, t), F32), pltpu.VMEM((2, t, t), F32),
                        pltpu.VMEM((2, 1, t), F32), pltpu.VMEM((2, 1, t), F32),
                        pltpu.VMEM((2, 1, t), F32), pltpu.VMEM((2, MLA_VT_ROWS, t), F32)],
        compiler_params=_params("parallel", "parallel", "arbitrary"),
        name="mla_attention",
    )(qt, k, vt)


def _outproj_ffn_kernel(x_ref, a_ref, m_ref, wo_ref, gm_ref, bm_ref, win32_ref, wout32_ref, g_ref, b_ref,
                        o_ref, win_ref, wout_ref):
    step = pl.program_id(0)

    @pl.when(step < FFN_CAST_STEPS)
    def _cast():
        _cast_ffn_slab(step, win32_ref, wout32_ref, win_ref, wout_ref)

    @pl.when(step >= FFN_CAST_STEPS)
    def _compute():
        width = a_ref.shape[1]
        mixed = _dot(a_ref[...], wo_ref[:width, :]) + _dot(m_ref[...], wo_ref[width:, :])
        x = _layer_norm(DEEPNORM_ALPHA * x_ref[...] + mixed, gm_ref[...], bm_ref[...])
        o_ref[...] = _ffn_block(x, win_ref, wout_ref, g_ref, b_ref)


def _outproj_ffn(x, attn, omem, w_out, g_mix, b_mix, w_in, w_ffn_out, layer, g, b):
    t, d = x.shape
    tm = min(ROW_TILE, t)
    vec = lambda a: a.reshape(1, d)
    return pl.pallas_call(
        _outproj_ffn_kernel,
        grid=(FFN_CAST_STEPS + t // tm,),
        in_specs=([_row_tile_spec(tm, d), _row_tile_spec(tm, attn.shape[1]), _row_tile_spec(tm, omem.shape[1]),
                   _resident(w_out.shape), _resident((1, d)), _resident((1, d))]
                  + _ffn_weight_specs(layer, d) + [_resident((1, d)), _resident((1, d))]),
        out_specs=_row_tile_spec(tm, d),
        out_shape=jax.ShapeDtypeStruct((t, d), F32),
        scratch_shapes=_ffn_weight_scratch(d),
        compiler_params=_params("arbitrary"),
        name="outproj_ffn",
    )(x, attn, omem, w_out, vec(g_mix), vec(b_mix), w_in, w_ffn_out, vec(g), vec(b))


def _rotate_half_columns(w):
    half = w.shape[1] // 2
    return jnp.concatenate([-w[:, half:], w[:, :half]], axis=1)


def _head_tiles(cols_per_head, offset):
    k, h, c = cols_per_head.shape
    tiles = jnp.zeros((k, h, LANES), cols_per_head.dtype).at[:, :, offset:offset + c].set(cols_per_head)
    return tiles.reshape(k, h * LANES)


def _mla_weights(w_in, q_norm_g, w_uq, kv_norm_g, w_ukv):
    q_dim = MLA_NOPE_DIM + MLA_ROPE_DIM
    kr0 = MLA_Q_RANK + MLA_KV_RANK
    w_kr = w_in[:, kr0:kr0 + MLA_ROPE_DIM]
    uq = w_uq.reshape(MLA_Q_RANK, N_MLA_HEADS, q_dim)
    ukv = w_ukv.reshape(MLA_KV_RANK, N_MLA_HEADS, MLA_NOPE_DIM + MLA_V_DIM)
    as_tile = lambda w, off: _head_tiles(w[:, None, :], off)
    w = {
        "cq": w_in[:, :MLA_Q_RANK],
        "ckv": w_in[:, MLA_Q_RANK:kr0],
        "kr": as_tile(w_kr, MLA_NOPE_DIM),
        "kr_rot": as_tile(_rotate_half_columns(w_kr), MLA_NOPE_DIM),
        "qmem": w_in[:, kr0 + MLA_ROPE_DIM:],
        "q": _head_tiles(uq, 0),
        "k": _head_tiles(ukv[:, :, :MLA_NOPE_DIM], 0),
        "v": ukv[:, :, MLA_NOPE_DIM:].reshape(MLA_KV_RANK, MLA_WIDTH),
    }
    w = {name: a.astype(BF16) for name, a in w.items()}
    w["gq"] = q_norm_g.reshape(1, MLA_Q_RANK)
    w["gkv"] = kv_norm_g.reshape(1, MLA_KV_RANK)
    return w


def _rope_tables(s):
    freqs = ROPE_BASE ** (-jnp.arange(0, MLA_ROPE_DIM, 2, dtype=F32) / MLA_ROPE_DIM)
    ang = jnp.arange(s).astype(F32)[:, None] * freqs[None, :]
    pad = jnp.zeros((s, LANES - MLA_NOPE_DIM - MLA_ROPE_DIM), F32)
    cos = jnp.concatenate([jnp.ones((s, MLA_NOPE_DIM), F32), jnp.cos(ang), jnp.cos(ang), pad], axis=1)
    sin = jnp.concatenate([jnp.zeros((s, MLA_NOPE_DIM), F32), jnp.sin(ang), jnp.sin(ang), pad], axis=1)
    return cos, sin


def kernel(x, mem, ln_ffn1_g, ln_ffn1_b, ln_mix_g, ln_mix_b, ln_ffn2_g, ln_ffn2_b, ffn1_w_in, ffn1_w_out, ffn2_w_in, ffn2_w_out, sb_w_in, mla_w_in, mla_q_norm_g, mla_w_uq, mla_kv_norm_g, mla_w_ukv, mem_w_kv, w_out):
    b, s, d = x.shape
    flat = lambda a: a.reshape(b * s, a.shape[-1])
    cos, sin = _rope_tables(s)
    for i in range(DEPTH):
        x = _ffn_ln(flat(x), ffn1_w_in, ffn1_w_out, i, ln_ffn1_g[i], ln_ffn1_b[i]).reshape(b, s, d)
        km, vm = _mem_kv(mem, mem_w_kv[i].astype(BF16))
        j = i // 2
        if i % 2 == 0:
            qt, k, vt, omem = _sb_inproj(x, sb_w_in[j].astype(BF16), km, vm)
            attn = _sb_attention(qt, k, vt)
        else:
            w = _mla_weights(mla_w_in[j], mla_q_norm_g[j], mla_w_uq[j], mla_kv_norm_g[j], mla_w_ukv[j])
            qt, k, vt, omem = _mla_inproj(x, w, cos, sin, km, vm)
            attn = _mla_attention(qt, k, vt)
        x = _outproj_ffn(flat(x), flat(attn), flat(omem), w_out[i].astype(BF16), ln_mix_g[i], ln_mix_b[i],
                         ffn2_w_in, ffn2_w_out, i, ln_ffn2_g[i], ln_ffn2_b[i]).reshape(b, s, d)
    return x
```

```python
import math

import jax
import jax.numpy as jnp
from jax import lax
from jax.experimental import pallas as pl
from jax.experimental.pallas import tpu as pltpu

D_MODEL = 1024
DEPTH = 2
HEAD_DIM = 64
N_SB_HEADS = 12
N_MLA_HEADS = 12
MLA_NOPE_DIM = 64
MLA_ROPE_DIM = 32
MLA_V_DIM = 64
MLA_Q_RANK = 384
MLA_KV_RANK = 256
ROPE_BASE = 10000.0
N_MEM_HEADS = 4
MEM_HEAD_DIM = 64
MEM_WIDTH = N_MEM_HEADS * MEM_HEAD_DIM
SB_WIDTH = N_SB_HEADS * HEAD_DIM
MLA_WIDTH = N_MLA_HEADS * MLA_V_DIM
D_FF = 2816
LN_EPS = 1e-5
RMS_EPS = 1e-6
DEEPNORM_ALPHA = (2 * DEPTH) ** 0.25

LANES = 128
VMEM_LIMIT_BYTES = 56 * 1024 * 1024

ROW_TILE = 512
PROJ_ROW_TILE = 1024
MXU_TILE = 256
FFN_SPLIT = (0, 6 * MXU_TILE, D_FF)
FFN_CAST_STEPS = D_FF // MXU_TILE
BF16_ROWS = 16
MLA_VT_ROWS = MLA_V_DIM + BF16_ROWS
SB_QUERY_TILE = 2048
SB_SUB = LANES
SB_WINDOW = 2 * SB_SUB
MLA_TILE = 512
MLA_UNROLL = 8
SB_LOG_WEIGHT_FLOOR = -88.0
MLA_LOG2_SCALE = math.log2(math.e) / math.sqrt(MLA_NOPE_DIM + MLA_ROPE_DIM)

BF16 = jnp.bfloat16
F32 = jnp.float32


def _dot(a, b):
    return jnp.dot(a, b, preferred_element_type=F32)


def _dot_nt(a, b):
    return lax.dot_general(a, b, (((1,), (1,)), ((), ())), preferred_element_type=F32)


def _layer_norm(y, g, b):
    mu = jnp.mean(y, axis=-1, keepdims=True)
    d = y - mu
    var = jnp.mean(d * d, axis=-1, keepdims=True)
    return d * lax.rsqrt(var + LN_EPS) * g + b


def _rms_norm(x, g):
    return x * lax.rsqrt(jnp.mean(x * x, axis=-1, keepdims=True) + RMS_EPS) * g


def _first_head(shape, axis):
    return lax.broadcasted_iota(jnp.int32, shape, axis) < HEAD_DIM


def _params(*semantics):
    return pltpu.CompilerParams(dimension_semantics=semantics, vmem_limit_bytes=VMEM_LIMIT_BYTES)


def _resident(shape):
    nd = len(shape)
    return pl.BlockSpec(shape, lambda *_: (0,) * nd, pipeline_mode=pl.Buffered(1))


def _ffn_block(x, win_ref, wout_ref, g_ref, b_ref):
    xb = x.astype(BF16)
    acc = None
    for lo, hi in zip(FFN_SPLIT[:-1], FFN_SPLIT[1:]):
        gate = _dot(xb, win_ref[:, lo:hi])
        up = _dot(xb, win_ref[:, D_FF + lo:D_FF + hi])
        act = (gate * jax.nn.sigmoid(gate) * up).astype(BF16)
        part = _dot(act, wout_ref[lo:hi, :])
        acc = part if acc is None else acc + part
    return _layer_norm(DEEPNORM_ALPHA * x + 0.5 * acc, g_ref[...], b_ref[...])


def _cast_ffn_slab(step, win32_ref, wout32_ref, win_ref, wout_ref):
    cols, rows = win32_ref.shape[1], wout32_ref.shape[0]
    win_ref[:, pl.ds(pl.multiple_of(step * cols, cols), cols)] = win32_ref[...].astype(BF16)
    wout_ref[pl.ds(pl.multiple_of(step * rows, rows), rows), :] = wout32_ref[...].astype(BF16)


def _ffn_weight_specs(layer, d):
    last = FFN_CAST_STEPS - 1
    return [pl.BlockSpec((None, d, 2 * D_FF // FFN_CAST_STEPS), lambda i: (layer, 0, jnp.minimum(i, last))),
            pl.BlockSpec((None, D_FF // FFN_CAST_STEPS, d), lambda i: (layer, jnp.minimum(i, last), 0))]


def _ffn_weight_scratch(d):
    return [pltpu.VMEM((d, 2 * D_FF), BF16), pltpu.VMEM((D_FF, d), BF16)]


def _row_tile_spec(tm, width):
    return pl.BlockSpec((tm, width), lambda i: (jnp.maximum(i - FFN_CAST_STEPS, 0), 0))


def _ffn_ln_kernel(x_ref, win32_ref, wout32_ref, g_ref, b_ref, o_ref, win_ref, wout_ref):
    step = pl.program_id(0)

    @pl.when(step < FFN_CAST_STEPS)
    def _cast():
        _cast_ffn_slab(step, win32_ref, wout32_ref, win_ref, wout_ref)

    @pl.when(step >= FFN_CAST_STEPS)
    def _compute():
        o_ref[...] = _ffn_block(x_ref[...], win_ref, wout_ref, g_ref, b_ref)


def _ffn_ln(x, w_in, w_out, layer, g, b):
    t, d = x.shape
    tm = min(ROW_TILE, t)
    return pl.pallas_call(
        _ffn_ln_kernel,
        grid=(FFN_CAST_STEPS + t // tm,),
        in_specs=[_row_tile_spec(tm, d)] + _ffn_weight_specs(layer, d) + [_resident((1, d)), _resident((1, d))],
        out_specs=_row_tile_spec(tm, d),
        out_shape=jax.ShapeDtypeStruct((t, d), F32),
        scratch_shapes=_ffn_weight_scratch(d),
        compiler_params=_params("arbitrary"),
        name="ffn_ln",
    )(x, w_in, w_out, g.reshape(1, d), b.reshape(1, d))


def _mem_kv_kernel(mem_ref, w_ref, k_ref, v_ref):
    kv = _dot(mem_ref[...].astype(BF16), w_ref[...])
    k_ref[...] = kv[:, :MEM_WIDTH].astype(BF16)
    v_ref[...] = kv[:, MEM_WIDTH:].astype(BF16)


def _mem_kv(mem, w_kv):
    b, m, d = mem.shape
    out = pl.BlockSpec((None, m, MEM_WIDTH), lambda i: (i, 0, 0))
    return pl.pallas_call(
        _mem_kv_kernel,
        grid=(b,),
        in_specs=[pl.BlockSpec((None, m, d), lambda i: (i, 0, 0)), _resident(w_kv.shape)],
        out_specs=[out, out],
        out_shape=[jax.ShapeDtypeStruct((b, m, MEM_WIDTH), BF16)] * 2,
        compiler_params=_params("parallel"),
        name="mem_kv",
    )(mem, w_kv)


def _memory_attention(q_mem, km_ref, vm_ref):
    tm = q_mem.shape[0]
    low = _first_head((tm, LANES), 1)
    scale = 1.0 / math.sqrt(MEM_HEAD_DIM)
    outs = []
    for p in range(MEM_WIDTH // LANES):
        q2 = q_mem[:, p * LANES:(p + 1) * LANES]
        k2 = km_ref[:, p * LANES:(p + 1) * LANES]
        v2 = vm_ref[:, p * LANES:(p + 1) * LANES]
        pair = None
        for own in (low, jnp.logical_not(low)):
            qh = jnp.where(own, q2, 0.0).astype(BF16)
            s = _dot_nt(qh, k2) * scale
            e = jnp.exp(s - jnp.max(s, axis=-1, keepdims=True))
            prob = e / jnp.sum(e, axis=-1, keepdims=True)
            o = _dot(prob.astype(BF16), v2)
            pair = o if pair is None else jnp.where(low, pair, o)
        outs.append(pair)
    return jnp.concatenate(outs, axis=-1)


def _sb_inproj_kernel(x_ref, wqt_ref, wk_ref, wvt_ref, wqm_ref, km_ref, vm_ref,
                      qt_ref, k_ref, vt_ref, omem_ref):
    xb = x_ref[...].astype(BF16)
    qt_ref[...] = (_dot_nt(wqt_ref[...], xb) * (1.0 / math.sqrt(HEAD_DIM))).astype(BF16)
    k_ref[...] = _dot(xb, wk_ref[...]).astype(BF16)
    vt_ref[...] = _dot_nt(wvt_ref[...], xb).astype(BF16)
    omem_ref[...] = _memory_attention(_dot(xb, wqm_ref[...]), km_ref, vm_ref).astype(BF16)


def _sb_inproj(x, w_in, km, vm):
    b, s, d = x.shape
    m = km.shape[1]
    tm = min(PROJ_ROW_TILE, s)
    w = SB_WIDTH
    weights = [w_in[:, :w].T, w_in[:, w:2 * w], w_in[:, 2 * w:3 * w].T, w_in[:, 3 * w:]]
    mem = pl.BlockSpec((None, m, MEM_WIDTH), lambda bi, i: (bi, 0, 0))
    rows = lambda width: pl.BlockSpec((None, tm, width), lambda bi, i: (bi, i, 0))
    cols = pl.BlockSpec((None, w, tm), lambda bi, i: (bi, 0, i))
    return pl.pallas_call(
        _sb_inproj_kernel,
        grid=(b, s // tm),
        in_specs=[rows(d)] + [_resident(a.shape) for a in weights] + [mem, mem],
        out_specs=[cols, rows(w), cols, rows(MEM_WIDTH)],
        out_shape=[jax.ShapeDtypeStruct((b, w, s), BF16), jax.ShapeDtypeStruct((b, s, w), BF16),
                   jax.ShapeDtypeStruct((b, w, s), BF16), jax.ShapeDtypeStruct((b, s, MEM_WIDTH), BF16)],
        compiler_params=_params("parallel", "parallel"),
        name="sb_inproj",
    )(x, *weights, km, vm)


def _split3(x):
    hi = x.astype(BF16)
    r = x - hi.astype(F32)
    mid = r.astype(BF16)
    lo = (r - mid.astype(F32)).astype(BF16)
    return hi, mid, lo


def _later3(n):
    row = lax.broadcasted_iota(jnp.int32, (n, n), 0)
    col = lax.broadcasted_iota(jnp.int32, (n, n), 1)
    later = jnp.where(col > row, 1.0, 0.0).astype(BF16)
    return jnp.concatenate([later, later, later], axis=1)


def _sb_logs(z, visible):
    decay = jnp.log(1.0 + jnp.exp2(jnp.abs(z) * -math.log2(math.e))) + jnp.maximum(z, 0.0)
    return z - decay, jnp.where(visible, decay, 0.0)


def _sb_behind(later3, decay):
    return _dot(later3, jnp.concatenate(_split3(decay), axis=0))


def _sb_chain(qt, k_ref, vt_ref, start, n_keys, first_query, tail_in):
    key = lax.broadcasted_iota(jnp.int32, (n_keys, SB_SUB), 0)
    qry = lax.broadcasted_iota(jnp.int32, (n_keys, SB_SUB), 1)
    z = _dot(k_ref[pl.ds(start, n_keys), :], qt)
    visible = (key - qry) < (first_query - start)
    log_beta, decay = _sb_logs(z, visible)
    behind = _sb_behind(_later3(n_keys), decay)
    w = jnp.where(visible, jnp.exp(log_beta - behind + tail_in), 0.0)
    out_t = _dot(vt_ref[:, pl.ds(start, n_keys)], w.astype(BF16))
    return out_t, tail_in - jnp.sum(decay, axis=0, keepdims=True)


def _sb_attn_kernel(qt_ref, k_ref, vt_ref, o_ref, acc_ref, tail_ref):
    tq = qt_ref.shape[1]
    n_sub = tq // SB_SUB
    q0 = pl.program_id(2) * tq
    first = _first_head((LANES, SB_SUB), 0)
    owners = (first, jnp.logical_not(first))
    subs = [slice(u * SB_SUB, (u + 1) * SB_SUB) for u in range(n_sub)]

    def head_queries(u, h):
        return jnp.where(owners[h], qt_ref[:, subs[u]].astype(F32), 0.0).astype(BF16)

    starts = [pl.multiple_of(jnp.maximum(q0 + (u - 1) * SB_SUB, 0), SB_SUB) for u in range(n_sub)]
    later3 = _later3(SB_WINDOW)
    key = lax.broadcasted_iota(jnp.int32, (SB_WINDOW, 2 * SB_SUB), 0)
    qry = lax.broadcasted_iota(jnp.int32, (SB_WINDOW, 2 * SB_SUB), 1) & (SB_SUB - 1)
    ahead = key - qry
    zs = [_dot(k_ref[pl.ds(starts[u], SB_WINDOW), :],
               jnp.concatenate([head_queries(u, 0), head_queries(u, 1)], axis=1)) for u in range(n_sub)]
    stage = []
    for u in range(n_sub):
        visible = ahead < (q0 + u * SB_SUB - starts[u])
        log_beta, decay = _sb_logs(zs[u], visible)
        stage.append((visible, log_beta, decay, _sb_behind(later3, decay)))
    tail_max = {}
    pending = jnp.float32(-jnp.inf)
    for u in range(n_sub):
        visible, log_beta, decay, behind = stage[u]
        w = jnp.where(visible, jnp.exp(log_beta - behind), 0.0)
        out_t = _dot(vt_ref[:, pl.ds(starts[u], SB_WINDOW)], w.astype(BF16))
        tail = -jnp.sum(decay, axis=0, keepdims=True)
        for h in range(2):
            acc_ref[h, :, subs[u]] = out_t[:, h * SB_SUB:(h + 1) * SB_SUB]
            tail_ref[h, :, subs[u]] = tail[:, h * SB_SUB:(h + 1) * SB_SUB]
            tail_max[u, h] = jnp.max(tail[:, h * SB_SUB:(h + 1) * SB_SUB])
            pending = jnp.maximum(pending, jnp.where(starts[u] > 0, tail_max[u, h], -jnp.inf))

    @pl.when(pending > SB_LOG_WEIGHT_FLOOR)
    def _walk_further_back():
        for u in range(n_sub):
            for h in range(2):
                qt = head_queries(u, h)

                def cond(state):
                    start, tail_max = state
                    return jnp.logical_and(start > 0, tail_max > SB_LOG_WEIGHT_FLOOR)

                def body(state, qt=qt, u=u, h=h):
                    start = pl.multiple_of(state[0] - SB_SUB, SB_SUB)
                    out_t, tail = _sb_chain(qt, k_ref, vt_ref, start, SB_SUB,
                                            q0 + u * SB_SUB, tail_ref[h, :, subs[u]])
                    acc_ref[h, :, subs[u]] += out_t
                    tail_ref[h, :, subs[u]] = tail
                    return start, jnp.max(tail)

                lax.while_loop(cond, body, (starts[u], tail_max[u, h]))

    out_t = jnp.where(_first_head((LANES, tq), 0), acc_ref[0], acc_ref[1])
    o_ref[...] = out_t.astype(o_ref.dtype)


def _sb_attention(qt, k, vt):
    b, s, _ = k.shape
    tq = min(SB_QUERY_TILE, s)
    pairs = SB_WIDTH // LANES
    return pl.pallas_call(
        _sb_attn_kernel,
        grid=(b, pairs, s // tq),
        in_specs=[pl.BlockSpec((None, LANES, tq), lambda bi, p, i: (bi, p, i)),
                  pl.BlockSpec((None, s, LANES), lambda bi, p, i: (bi, 0, p)),
                  pl.BlockSpec((None, LANES, s), lambda bi, p, i: (bi, p, 0))],
        out_specs=pl.BlockSpec((None, LANES, tq), lambda bi, p, i: (bi, p, i)),
        out_shape=jax.ShapeDtypeStruct((b, SB_WIDTH, s), BF16),
        scratch_shapes=[pltpu.VMEM((2, LANES, tq), F32), pltpu.VMEM((2, 1, tq), F32)],
        compiler_params=_params("parallel", "parallel", "arbitrary"),
        name="sb_attention",
    )(qt, k, vt)


def _mla_inproj_kernel(x_ref, wcq_ref, wckv_ref, wkr_ref, wkrr_ref, wqm_ref, gq_ref, gkv_ref,
                       wqt_ref, wk_ref, wvt_ref, cos_ref, sin_ref, cost_ref, sint_ref,
                       km_ref, vm_ref, qt_ref, k_ref, vt_ref, omem_ref):
    xb = x_ref[...].astype(BF16)
    cos, sin = cos_ref[...], sin_ref[...]
    cos_t, sin_t = cost_ref[...], sint_ref[...]
    c_q = _rms_norm(_dot(xb, wcq_ref[...]), gq_ref[...]).astype(BF16)
    c_kv = _rms_norm(_dot(xb, wckv_ref[...]), gkv_ref[...]).astype(BF16)
    k_rope = _dot(xb, wkr_ref[...]) * cos + _dot(xb, wkrr_ref[...]) * sin
    k_rope2 = jnp.concatenate([k_rope, k_rope], axis=-1)
    r0, r1, r2 = MLA_NOPE_DIM, MLA_NOPE_DIM + MLA_ROPE_DIM // 2, MLA_NOPE_DIM + MLA_ROPE_DIM
    for p in range(N_MLA_HEADS // 2):
        lo, hi = 2 * p * LANES, 2 * (p + 1) * LANES
        q_pair = _dot_nt(wqt_ref[lo:hi, :], c_q)
        for h in range(2):
            q_t = q_pair[h * LANES:(h + 1) * LANES]
            rotated = jnp.concatenate([q_t[:r0], -q_t[r1:r2], q_t[r0:r1], q_t[r2:]], axis=0)
            q_t = (q_t * cos_t + rotated * sin_t) * MLA_LOG2_SCALE
            qt_ref[lo + h * LANES:lo + (h + 1) * LANES, :] = q_t.astype(BF16)
        k_ref[:, lo:hi] = (_dot(c_kv, wk_ref[:, lo:hi]) + k_rope2).astype(BF16)
    v_t = _dot_nt(wvt_ref[...], c_kv).astype(BF16)
    ones = jnp.ones((MLA_VT_ROWS - MLA_V_DIM, v_t.shape[1]), BF16)
    for h in range(N_MLA_HEADS):
        vt_ref[h * MLA_VT_ROWS:h * MLA_VT_ROWS + MLA_V_DIM, :] = v_t[h * MLA_V_DIM:(h + 1) * MLA_V_DIM, :]
        vt_ref[h * MLA_VT_ROWS + MLA_V_DIM:(h + 1) * MLA_VT_ROWS, :] = ones
    omem_ref[...] = _memory_attention(_dot(xb, wqm_ref[...]), km_ref, vm_ref).astype(BF16)


def _mla_inproj(x, w, cos, sin, km, vm):
    b, s, d = x.shape
    m = km.shape[1]
    tm = min(PROJ_ROW_TILE, s)
    hw = N_MLA_HEADS * LANES
    mem = pl.BlockSpec((None, m, MEM_WIDTH), lambda bi, i: (bi, 0, 0))
    table = pl.BlockSpec((tm, LANES), lambda bi, i: (i, 0))
    table_t = pl.BlockSpec((LANES, tm), lambda bi, i: (0, i))
    rows = lambda width: pl.BlockSpec((None, tm, width), lambda bi, i: (bi, i, 0))
    cols = lambda width: pl.BlockSpec((None, width, tm), lambda bi, i: (bi, 0, i))
    weights = [w["cq"], w["ckv"], w["kr"], w["kr_rot"], w["qmem"], w["gq"], w["gkv"],
               w["q"].T, w["k"], w["v"].T]
    return pl.pallas_call(
        _mla_inproj_kernel,
        grid=(b, s // tm),
        in_specs=([rows(d)] + [_resident(a.shape) for a in weights]
                  + [table, table, table_t, table_t, mem, mem]),
        out_specs=[cols(hw), rows(hw), cols(N_MLA_HEADS * MLA_VT_ROWS), rows(MEM_WIDTH)],
        out_shape=[jax.ShapeDtypeStruct((b, hw, s), BF16), jax.ShapeDtypeStruct((b, s, hw), BF16),
                   jax.ShapeDtypeStruct((b, N_MLA_HEADS * MLA_VT_ROWS, s), BF16),
                   jax.ShapeDtypeStruct((b, s, MEM_WIDTH), BF16)],
        compiler_params=_params("parallel", "parallel"),
        name="mla_inproj",
    )(x, *weights, cos, sin, cos.T, sin.T, km, vm)


def _mla_attn_kernel(qt_ref, k_ref, vt_ref, o_ref, s0_ref, s1_ref, b0_ref, b1_ref, m_ref, acc_ref):
    t = qt_ref.shape[1]
    i = pl.program_id(2)
    m_ref[...] = jnp.full_like(m_ref, -jnp.inf)
    acc_ref[...] = jnp.zeros_like(acc_ref)

    def scores(j, s_ref, b_ref):
        start = pl.multiple_of(j * t, t)
        for h in range(2):
            tile = slice(h * LANES, (h + 1) * LANES)
            s_t = _dot(k_ref[pl.ds(start, t), tile], qt_ref[tile, :])
            s_ref[h] = s_t
            b_ref[h] = jnp.max(s_t, axis=0, keepdims=True)

    def update(j, s_ref, b_ref, masked):
        start = pl.multiple_of(j * t, t)
        for h in range(2):
            s_t = s_ref[h]
            if masked:
                key = lax.broadcasted_iota(jnp.int32, (t, t), 0)
                qry = lax.broadcasted_iota(jnp.int32, (t, t), 1)
                s_t = jnp.where(key <= qry, s_t, -jnp.inf)
                block_max = jnp.max(s_t, axis=0, keepdims=True)
            else:
                block_max = b_ref[h]
            m_old = m_ref[h]
            m_new = jnp.maximum(m_old, block_max)
            alpha = jnp.exp2(m_old - m_new)
            prob_t = jnp.exp2(s_t - m_new)
            v_t = vt_ref[h * MLA_VT_ROWS:(h + 1) * MLA_VT_ROWS, pl.ds(start, t)]
            acc_ref[h] = alpha * acc_ref[h] + _dot(v_t, prob_t.astype(BF16))
            m_ref[h] = m_new

    def two_full_steps(j):
        scores(j + 1, s1_ref, b1_ref)
        update(j, s0_ref, b0_ref, masked=False)
        scores(j + 2, s0_ref, b0_ref)
        update(j + 1, s1_ref, b1_ref, masked=False)

    def eight_full_steps(jj, carry):
        for step in range(0, MLA_UNROLL, 2):
            two_full_steps(MLA_UNROLL * jj + step)
        return carry

    scores(0, s0_ref, b0_ref)
    lax.fori_loop(0, i // MLA_UNROLL, eight_full_steps, 0)
    span = MLA_UNROLL // 2
    while span >= 2:
        @pl.when(i % (2 * span) >= span)
        def _more_full_steps(span=span):
            for step in range(0, span, 2):
                two_full_steps((2 * span) * (i // (2 * span)) + step)
        span //= 2

    @pl.when(i % 2 == 0)
    def _diagonal_in_s0():
        update(i, s0_ref, b0_ref, masked=True)

    @pl.when(i % 2 == 1)
    def _last_full_then_diagonal():
        scores(i, s1_ref, b1_ref)
        update(i - 1, s0_ref, b0_ref, masked=False)
        update(i, s1_ref, b1_ref, masked=True)

    heads = [acc_ref[h, :MLA_V_DIM, :] / acc_ref[h, MLA_V_DIM:MLA_V_DIM + 1, :] for h in range(2)]
    o_ref[...] = jnp.concatenate(heads, axis=0).astype(o_ref.dtype)


def _mla_attention(qt, k, vt):
    b, s, _ = k.shape
    t = min(MLA_TILE, s)
    return pl.pallas_call(
        _mla_attn_kernel,
        grid=(b, N_MLA_HEADS // 2, s // t),
        in_specs=[pl.BlockSpec((None, 2 * LANES, t), lambda bi, p, i: (bi, p, i)),
                  pl.BlockSpec((None, s, 2 * LANES), lambda bi, p, i: (bi, 0, p)),
                  pl.BlockSpec((None, 2 * MLA_VT_ROWS, s), lambda bi, p, i: (bi, p, 0))],
        out_specs=pl.BlockSpec((None, LANES, t), lambda bi, p, i: (bi, p, i)),
        out_shape=jax.ShapeDtypeStruct((b, MLA_WIDTH, s), BF16),
        scratch_shapes=[pltpu.VMEM((2, t, t), F32), pltpu.VMEM((2, t, t), F32),
                        pltpu.VMEM((2, 1, t), F32), pltpu.VMEM((2, 1, t), F32),
                        pltpu.VMEM((2, 1, t), F32), pltpu.VMEM((2, MLA_VT_ROWS, t), F32)],
        compiler_params=_params("parallel", "parallel", "arbitrary"),
        name="mla_attention",
    )(qt, k, vt)


def _outproj_ffn_kernel(x_ref, a_ref, m_ref, wo_ref, gm_ref, bm_ref, win32_ref, wout32_ref, g_ref, b_ref,
                        o_ref, win_ref, wout_ref):
    step = pl.program_id(0)

    @pl.when(step < FFN_CAST_STEPS)
    def _cast():
        _cast_ffn_slab(step, win32_ref, wout32_ref, win_ref, wout_ref)

    @pl.when(step >= FFN_CAST_STEPS)
    def _compute():
        width = a_ref.shape[0]
        heads = lax.dot_general(a_ref[...], wo_ref[:width, :], (((0,), (0,)), ((), ())),
                                preferred_element_type=F32)
        mixed = heads + _dot(m_ref[...], wo_ref[width:, :])
        x = _layer_norm(DEEPNORM_ALPHA * x_ref[...] + mixed, gm_ref[...], bm_ref[...])
        o_ref[...] = _ffn_block(x, win_ref, wout_ref, g_ref, b_ref)


def _outproj_ffn(x, attn_t, omem, w_out, g_mix, b_mix, w_in, w_ffn_out, layer, g, b):
    t, d = x.shape
    tm = min(ROW_TILE, t)
    width, s = attn_t.shape[1:]
    per_batch = s // tm
    assert per_batch * tm == s
    vec = lambda a: a.reshape(1, d)

    def attn_index(i):
        tile = jnp.maximum(i - FFN_CAST_STEPS, 0)
        return tile // per_batch, 0, tile % per_batch

    return pl.pallas_call(
        _outproj_ffn_kernel,
        grid=(FFN_CAST_STEPS + t // tm,),
        in_specs=([_row_tile_spec(tm, d), pl.BlockSpec((None, width, tm), attn_index),
                   _row_tile_spec(tm, omem.shape[1]),
                   _resident(w_out.shape), _resident((1, d)), _resident((1, d))]
                  + _ffn_weight_specs(layer, d) + [_resident((1, d)), _resident((1, d))]),
        out_specs=_row_tile_spec(tm, d),
        out_shape=jax.ShapeDtypeStruct((t, d), F32),
        scratch_shapes=_ffn_weight_scratch(d),
        compiler_params=_params("arbitrary"),
        name="outproj_ffn",
    )(x, attn_t, omem, w_out, vec(g_mix), vec(b_mix), w_in, w_ffn_out, vec(g), vec(b))


def _rotate_half_columns(w):
    half = w.shape[1] // 2
    return jnp.concatenate([-w[:, half:], w[:, :half]], axis=1)


def _head_tiles(cols_per_head, offset):
    k, h, c = cols_per_head.shape
    tiles = jnp.zeros((k, h, LANES), cols_per_head.dtype).at[:, :, offset:offset + c].set(cols_per_head)
    return tiles.reshape(k, h * LANES)


def _mla_weights(w_in, q_norm_g, w_uq, kv_norm_g, w_ukv):
    q_dim = MLA_NOPE_DIM + MLA_ROPE_DIM
    kr0 = MLA_Q_RANK + MLA_KV_RANK
    w_kr = w_in[:, kr0:kr0 + MLA_ROPE_DIM]
    uq = w_uq.reshape(MLA_Q_RANK, N_MLA_HEADS, q_dim)
    ukv = w_ukv.reshape(MLA_KV_RANK, N_MLA_HEADS, MLA_NOPE_DIM + MLA_V_DIM)
    as_tile = lambda w, off: _head_tiles(w[:, None, :], off)
    w = {
        "cq": w_in[:, :MLA_Q_RANK],
        "ckv": w_in[:, MLA_Q_RANK:kr0],
        "kr": as_tile(w_kr, MLA_NOPE_DIM),
        "kr_rot": as_tile(_rotate_half_columns(w_kr), MLA_NOPE_DIM),
        "qmem": w_in[:, kr0 + MLA_ROPE_DIM:],
        "q": _head_tiles(uq, 0),
        "k": _head_tiles(ukv[:, :, :MLA_NOPE_DIM], 0),
        "v": ukv[:, :, MLA_NOPE_DIM:].reshape(MLA_KV_RANK, MLA_WIDTH),
    }
    w = {name: a.astype(BF16) for name, a in w.items()}
    w["gq"] = q_norm_g.reshape(1, MLA_Q_RANK)
    w["gkv"] = kv_norm_g.reshape(1, MLA_KV_RANK)
    return w


def _rope_tables(s):
    freqs = ROPE_BASE ** (-jnp.arange(0, MLA_ROPE_DIM, 2, dtype=F32) / MLA_ROPE_DIM)
    ang = jnp.arange(s).astype(F32)[:, None] * freqs[None, :]
    pad = jnp.zeros((s, LANES - MLA_NOPE_DIM - MLA_ROPE_DIM), F32)
    cos = jnp.concatenate([jnp.ones((s, MLA_NOPE_DIM), F32), jnp.cos(ang), jnp.cos(ang), pad], axis=1)
    sin = jnp.concatenate([jnp.zeros((s, MLA_NOPE_DIM), F32), jnp.sin(ang), jnp.sin(ang), pad], axis=1)
    return cos, sin


def kernel(x, mem, ln_ffn1_g, ln_ffn1_b, ln_mix_g, ln_mix_b, ln_ffn2_g, ln_ffn2_b, ffn1_w_in, ffn1_w_out, ffn2_w_in, ffn2_w_out, sb_w_in, mla_w_in, mla_q_norm_g, mla_w_uq, mla_kv_norm_g, mla_w_ukv, mem_w_kv, w_out):
    b, s, d = x.shape
    flat = lambda a: a.reshape(b * s, a.shape[-1])
    cos, sin = _rope_tables(s)
    for i in range(DEPTH):
        x = _ffn_ln(flat(x), ffn1_w_in, ffn1_w_out, i, ln_ffn1_g[i], ln_ffn1_b[i]).reshape(b, s, d)
        km, vm = _mem_kv(mem, mem_w_kv[i].astype(BF16))
        j = i // 2
        if i % 2 == 0:
            qt, k, vt, omem = _sb_inproj(x, sb_w_in[j].astype(BF16), km, vm)
            attn = _sb_attention(qt, k, vt)
        else:
            w = _mla_weights(mla_w_in[j], mla_q_norm_g[j], mla_w_uq[j], mla_kv_norm_g[j], mla_w_ukv[j])
            qt, k, vt, omem = _mla_inproj(x, w, cos, sin, km, vm)
            attn = _mla_attention(qt, k, vt)
        x = _outproj_ffn(flat(x), attn, flat(omem), w_out[i].astype(BF16), ln_mix_g[i], ln_mix_b[i],
                         ffn2_w_in, ffn2_w_out, i, ln_ffn2_g[i], ln_ffn2_b[i]).reshape(b, s, d)
    return x
```

```python
import math

import jax
import jax.numpy as jnp
from jax import lax
from jax.experimental import pallas as pl
from jax.experimental.pallas import tpu as pltpu

D_MODEL = 1024
DEPTH = 2
HEAD_DIM = 64
N_SB_HEADS = 12
N_MLA_HEADS = 12
MLA_NOPE_DIM = 64
MLA_ROPE_DIM = 32
MLA_V_DIM = 64
MLA_Q_RANK = 384
MLA_KV_RANK = 256
ROPE_BASE = 10000.0
N_MEM_HEADS = 4
MEM_HEAD_DIM = 64
MEM_WIDTH = N_MEM_HEADS * MEM_HEAD_DIM
SB_WIDTH = N_SB_HEADS * HEAD_DIM
MLA_WIDTH = N_MLA_HEADS * MLA_V_DIM
D_FF = 2816
LN_EPS = 1e-5
RMS_EPS = 1e-6
DEEPNORM_ALPHA = (2 * DEPTH) ** 0.25

LANES = 128
VMEM_LIMIT_BYTES = 56 * 1024 * 1024

ROW_TILE = 512
PROJ_ROW_TILE = 1024
MXU_TILE = 256
FFN_SPLIT = (0, 6 * MXU_TILE, D_FF)
FFN_CAST_STEPS = D_FF // MXU_TILE
BF16_ROWS = 16
MLA_VT_ROWS = MLA_V_DIM + BF16_ROWS
SB_QUERY_TILE = 2048
SB_SUB = LANES
SB_WINDOW = 2 * SB_SUB
MLA_TILE = 512
MLA_UNROLL = 16
SB_LOG_WEIGHT_FLOOR = -88.0
MLA_LOG2_SCALE = math.log2(math.e) / math.sqrt(MLA_NOPE_DIM + MLA_ROPE_DIM)

BF16 = jnp.bfloat16
F32 = jnp.float32


def _dot(a, b):
    return jnp.dot(a, b, preferred_element_type=F32)


def _dot_nt(a, b):
    return lax.dot_general(a, b, (((1,), (1,)), ((), ())), preferred_element_type=F32)


def _layer_norm(y, g, b):
    mu = jnp.mean(y, axis=-1, keepdims=True)
    d = y - mu
    var = jnp.mean(d * d, axis=-1, keepdims=True)
    return d * lax.rsqrt(var + LN_EPS) * g + b


def _rms_norm(x, g):
    return x * lax.rsqrt(jnp.mean(x * x, axis=-1, keepdims=True) + RMS_EPS) * g


def _first_head(shape, axis):
    return lax.broadcasted_iota(jnp.int32, shape, axis) < HEAD_DIM


def _params(*semantics):
    return pltpu.CompilerParams(dimension_semantics=semantics, vmem_limit_bytes=VMEM_LIMIT_BYTES)


def _resident(shape):
    nd = len(shape)
    return pl.BlockSpec(shape, lambda *_: (0,) * nd, pipeline_mode=pl.Buffered(1))


def _ffn_block(x, win_ref, wout_ref, g_ref, b_ref):
    xb = x.astype(BF16)
    acc = None
    for lo, hi in zip(FFN_SPLIT[:-1], FFN_SPLIT[1:]):
        gate = _dot(xb, win_ref[:, lo:hi])
        up = _dot(xb, win_ref[:, D_FF + lo:D_FF + hi])
        act = (gate * jax.nn.sigmoid(gate) * up).astype(BF16)
        part = _dot(act, wout_ref[lo:hi, :])
        acc = part if acc is None else acc + part
    return _layer_norm(DEEPNORM_ALPHA * x + 0.5 * acc, g_ref[...], b_ref[...])


def _cast_ffn_slab(step, win32_ref, wout32_ref, win_ref, wout_ref):
    cols, rows = win32_ref.shape[1], wout32_ref.shape[0]
    win_ref[:, pl.ds(pl.multiple_of(step * cols, cols), cols)] = win32_ref[...].astype(BF16)
    wout_ref[pl.ds(pl.multiple_of(step * rows, rows), rows), :] = wout32_ref[...].astype(BF16)


def _ffn_weight_specs(layer, d):
    last = FFN_CAST_STEPS - 1
    return [pl.BlockSpec((None, d, 2 * D_FF // FFN_CAST_STEPS), lambda i: (layer, 0, jnp.minimum(i, last))),
            pl.BlockSpec((None, D_FF // FFN_CAST_STEPS, d), lambda i: (layer, jnp.minimum(i, last), 0))]


def _ffn_weight_scratch(d):
    return [pltpu.VMEM((d, 2 * D_FF), BF16), pltpu.VMEM((D_FF, d), BF16)]


def _row_tile_spec(tm, width):
    return pl.BlockSpec((tm, width), lambda i: (jnp.maximum(i - FFN_CAST_STEPS, 0), 0))


def _ffn_ln_kernel(x_ref, win32_ref, wout32_ref, g_ref, b_ref, o_ref, win_ref, wout_ref):
    step = pl.program_id(0)

    @pl.when(step < FFN_CAST_STEPS)
    def _cast():
        _cast_ffn_slab(step, win32_ref, wout32_ref, win_ref, wout_ref)

    @pl.when(step >= FFN_CAST_STEPS)
    def _compute():
        o_ref[...] = _ffn_block(x_ref[...], win_ref, wout_ref, g_ref, b_ref)


def _ffn_ln(x, w_in, w_out, layer, g, b):
    t, d = x.shape
    tm = min(ROW_TILE, t)
    return pl.pallas_call(
        _ffn_ln_kernel,
        grid=(FFN_CAST_STEPS + t // tm,),
        in_specs=[_row_tile_spec(tm, d)] + _ffn_weight_specs(layer, d) + [_resident((1, d)), _resident((1, d))],
        out_specs=_row_tile_spec(tm, d),
        out_shape=jax.ShapeDtypeStruct((t, d), F32),
        scratch_shapes=_ffn_weight_scratch(d),
        compiler_params=_params("arbitrary"),
        name="ffn_ln",
    )(x, w_in, w_out, g.reshape(1, d), b.reshape(1, d))


def _mem_kv_kernel(mem_ref, w_ref, k_ref, v_ref):
    kv = _dot(mem_ref[...].astype(BF16), w_ref[...])
    k_ref[...] = kv[:, :MEM_WIDTH].astype(BF16)
    v_ref[...] = kv[:, MEM_WIDTH:].astype(BF16)


def _mem_kv(mem, w_kv):
    b, m, d = mem.shape
    out = pl.BlockSpec((None, m, MEM_WIDTH), lambda i: (i, 0, 0))
    return pl.pallas_call(
        _mem_kv_kernel,
        grid=(b,),
        in_specs=[pl.BlockSpec((None, m, d), lambda i: (i, 0, 0)), _resident(w_kv.shape)],
        out_specs=[out, out],
        out_shape=[jax.ShapeDtypeStruct((b, m, MEM_WIDTH), BF16)] * 2,
        compiler_params=_params("parallel"),
        name="mem_kv",
    )(mem, w_kv)


def _memory_attention(q_mem, km_ref, vm_ref):
    tm = q_mem.shape[0]
    low = _first_head((tm, LANES), 1)
    scale = 1.0 / math.sqrt(MEM_HEAD_DIM)
    outs = []
    for p in range(MEM_WIDTH // LANES):
        q2 = q_mem[:, p * LANES:(p + 1) * LANES]
        k2 = km_ref[:, p * LANES:(p + 1) * LANES]
        v2 = vm_ref[:, p * LANES:(p + 1) * LANES]
        pair = None
        for own in (low, jnp.logical_not(low)):
            qh = jnp.where(own, q2, 0.0).astype(BF16)
            s = _dot_nt(qh, k2) * scale
            e = jnp.exp(s - jnp.max(s, axis=-1, keepdims=True))
            prob = e / jnp.sum(e, axis=-1, keepdims=True)
            o = _dot(prob.astype(BF16), v2)
            pair = o if pair is None else jnp.where(low, pair, o)
        outs.append(pair)
    return jnp.concatenate(outs, axis=-1)


def _sb_inproj_kernel(x_ref, wqt_ref, wk_ref, wvt_ref, wqm_ref, km_ref, vm_ref,
                      qt_ref, k_ref, vt_ref, omem_ref):
    xb = x_ref[...].astype(BF16)
    qt_ref[...] = (_dot_nt(wqt_ref[...], xb) * (1.0 / math.sqrt(HEAD_DIM))).astype(BF16)
    k_ref[...] = _dot(xb, wk_ref[...]).astype(BF16)
    vt_ref[...] = _dot_nt(wvt_ref[...], xb).astype(BF16)
    omem_ref[...] = _memory_attention(_dot(xb, wqm_ref[...]), km_ref, vm_ref).astype(BF16)


def _sb_inproj(x, w_in, km, vm):
    b, s, d = x.shape
    m = km.shape[1]
    tm = min(PROJ_ROW_TILE, s)
    w = SB_WIDTH
    weights = [w_in[:, :w].T, w_in[:, w:2 * w], w_in[:, 2 * w:3 * w].T, w_in[:, 3 * w:]]
    mem = pl.BlockSpec((None, m, MEM_WIDTH), lambda bi, i: (bi, 0, 0))
    rows = lambda width: pl.BlockSpec((None, tm, width), lambda bi, i: (bi, i, 0))
    cols = pl.BlockSpec((None, w, tm), lambda bi, i: (bi, 0, i))
    return pl.pallas_call(
        _sb_inproj_kernel,
        grid=(b, s // tm),
        in_specs=[rows(d)] + [_resident(a.shape) for a in weights] + [mem, mem],
        out_specs=[cols, rows(w), cols, rows(MEM_WIDTH)],
        out_shape=[jax.ShapeDtypeStruct((b, w, s), BF16), jax.ShapeDtypeStruct((b, s, w), BF16),
                   jax.ShapeDtypeStruct((b, w, s), BF16), jax.ShapeDtypeStruct((b, s, MEM_WIDTH), BF16)],
        compiler_params=_params("parallel", "parallel"),
        name="sb_inproj",
    )(x, *weights, km, vm)


def _split3(x):
    hi = x.astype(BF16)
    r = x - hi.astype(F32)
    mid = r.astype(BF16)
    lo = (r - mid.astype(F32)).astype(BF16)
    return hi, mid, lo


def _later3(n):
    row = lax.broadcasted_iota(jnp.int32, (n, n), 0)
    col = lax.broadcasted_iota(jnp.int32, (n, n), 1)
    later = jnp.where(col > row, 1.0, 0.0).astype(BF16)
    return jnp.concatenate([later, later, later], axis=1)


def _sb_logs(z, visible):
    decay = jnp.log(1.0 + jnp.exp2(jnp.abs(z) * -math.log2(math.e))) + jnp.maximum(z, 0.0)
    return z - decay, jnp.where(visible, decay, 0.0)


def _sb_behind(later3, decay):
    return _dot(later3, jnp.concatenate(_split3(decay), axis=0))


def _sb_chain(qt, k_ref, vt_ref, start, n_keys, first_query, tail_in):
    key = lax.broadcasted_iota(jnp.int32, (n_keys, SB_SUB), 0)
    qry = lax.broadcasted_iota(jnp.int32, (n_keys, SB_SUB), 1)
    z = _dot(k_ref[pl.ds(start, n_keys), :], qt)
    visible = (key - qry) < (first_query - start)
    log_beta, decay = _sb_logs(z, visible)
    behind = _sb_behind(_later3(n_keys), decay)
    w = jnp.where(visible, jnp.exp(log_beta - behind + tail_in), 0.0)
    out_t = _dot(vt_ref[:, pl.ds(start, n_keys)], w.astype(BF16))
    return out_t, tail_in - jnp.sum(decay, axis=0, keepdims=True)


def _sb_attn_kernel(qt_ref, k_ref, vt_ref, o_ref, acc_ref, tail_ref):
    tq = qt_ref.shape[1]
    n_sub = tq // SB_SUB
    q0 = pl.program_id(2) * tq
    first = _first_head((LANES, SB_SUB), 0)
    owners = (first, jnp.logical_not(first))
    subs = [slice(u * SB_SUB, (u + 1) * SB_SUB) for u in range(n_sub)]

    def head_queries(u, h):
        return jnp.where(owners[h], qt_ref[:, subs[u]].astype(F32), 0.0).astype(BF16)

    starts = [pl.multiple_of(jnp.maximum(q0 + (u - 1) * SB_SUB, 0), SB_SUB) for u in range(n_sub)]
    later3 = _later3(SB_WINDOW)
    key = lax.broadcasted_iota(jnp.int32, (SB_WINDOW, 2 * SB_SUB), 0)
    qry = lax.broadcasted_iota(jnp.int32, (SB_WINDOW, 2 * SB_SUB), 1) & (SB_SUB - 1)
    ahead = key - qry
    zs = [_dot(k_ref[pl.ds(starts[u], SB_WINDOW), :],
               jnp.concatenate([head_queries(u, 0), head_queries(u, 1)], axis=1)) for u in range(n_sub)]
    stage = []
    for u in range(n_sub):
        visible = ahead < (q0 + u * SB_SUB - starts[u])
        log_beta, decay = _sb_logs(zs[u], visible)
        stage.append((visible, log_beta, decay, _sb_behind(later3, decay)))
    tail_max = {}
    pending = jnp.float32(-jnp.inf)
    for u in range(n_sub):
        visible, log_beta, decay, behind = stage[u]
        w = jnp.where(visible, jnp.exp(log_beta - behind), 0.0)
        out_t = _dot(vt_ref[:, pl.ds(starts[u], SB_WINDOW)], w.astype(BF16))
        tail = -jnp.sum(decay, axis=0, keepdims=True)
        for h in range(2):
            acc_ref[h, :, subs[u]] = out_t[:, h * SB_SUB:(h + 1) * SB_SUB]
            tail_ref[h, :, subs[u]] = tail[:, h * SB_SUB:(h + 1) * SB_SUB]
            tail_max[u, h] = jnp.max(tail[:, h * SB_SUB:(h + 1) * SB_SUB])
            pending = jnp.maximum(pending, jnp.where(starts[u] > 0, tail_max[u, h], -jnp.inf))

    @pl.when(pending > SB_LOG_WEIGHT_FLOOR)
    def _walk_further_back():
        for u in range(n_sub):
            for h in range(2):
                qt = head_queries(u, h)

                def cond(state):
                    start, tail_max = state
                    return jnp.logical_and(start > 0, tail_max > SB_LOG_WEIGHT_FLOOR)

                def body(state, qt=qt, u=u, h=h):
                    start = pl.multiple_of(state[0] - SB_SUB, SB_SUB)
                    out_t, tail = _sb_chain(qt, k_ref, vt_ref, start, SB_SUB,
                                            q0 + u * SB_SUB, tail_ref[h, :, subs[u]])
                    acc_ref[h, :, subs[u]] += out_t
                    tail_ref[h, :, subs[u]] = tail
                    return start, jnp.max(tail)

                lax.while_loop(cond, body, (starts[u], tail_max[u, h]))

    out_t = jnp.where(_first_head((LANES, tq), 0), acc_ref[0], acc_ref[1])
    o_ref[...] = out_t.astype(o_ref.dtype)


def _sb_attention(qt, k, vt):
    b, s, _ = k.shape
    tq = min(SB_QUERY_TILE, s)
    pairs = SB_WIDTH // LANES
    return pl.pallas_call(
        _sb_attn_kernel,
        grid=(b, pairs, s // tq),
        in_specs=[pl.BlockSpec((None, LANES, tq), lambda bi, p, i: (bi, p, i)),
                  pl.BlockSpec((None, s, LANES), lambda bi, p, i: (bi, 0, p)),
                  pl.BlockSpec((None, LANES, s), lambda bi, p, i: (bi, p, 0))],
        out_specs=pl.BlockSpec((None, LANES, tq), lambda bi, p, i: (bi, p, i)),
        out_shape=jax.ShapeDtypeStruct((b, SB_WIDTH, s), BF16),
        scratch_shapes=[pltpu.VMEM((2, LANES, tq), F32), pltpu.VMEM((2, 1, tq), F32)],
        compiler_params=_params("parallel", "parallel", "arbitrary"),
        name="sb_attention",
    )(qt, k, vt)


def _mla_inproj_kernel(x_ref, wcq_ref, wckv_ref, wkr_ref, wkrr_ref, wqm_ref, gq_ref, gkv_ref,
                       wqt_ref, wk_ref, wvt_ref, cos_ref, sin_ref, cost_ref, sint_ref,
                       km_ref, vm_ref, qt_ref, k_ref, vt_ref, omem_ref):
    xb = x_ref[...].astype(BF16)
    cos, sin = cos_ref[...], sin_ref[...]
    cos_t, sin_t = cost_ref[...], sint_ref[...]
    c_q = _rms_norm(_dot(xb, wcq_ref[...]), gq_ref[...]).astype(BF16)
    c_kv = _rms_norm(_dot(xb, wckv_ref[...]), gkv_ref[...]).astype(BF16)
    k_rope = _dot(xb, wkr_ref[...]) * cos + _dot(xb, wkrr_ref[...]) * sin
    k_rope2 = jnp.concatenate([k_rope, k_rope], axis=-1)
    r0, r1, r2 = MLA_NOPE_DIM, MLA_NOPE_DIM + MLA_ROPE_DIM // 2, MLA_NOPE_DIM + MLA_ROPE_DIM
    for p in range(N_MLA_HEADS // 2):
        lo, hi = 2 * p * LANES, 2 * (p + 1) * LANES
        q_pair = _dot_nt(wqt_ref[lo:hi, :], c_q)
        for h in range(2):
            q_t = q_pair[h * LANES:(h + 1) * LANES]
            rotated = jnp.concatenate([q_t[:r0], -q_t[r1:r2], q_t[r0:r1], q_t[r2:]], axis=0)
            q_t = (q_t * cos_t + rotated * sin_t) * MLA_LOG2_SCALE
            qt_ref[lo + h * LANES:lo + (h + 1) * LANES, :] = q_t.astype(BF16)
        k_ref[:, lo:hi] = (_dot(c_kv, wk_ref[:, lo:hi]) + k_rope2).astype(BF16)
    v_t = _dot_nt(wvt_ref[...], c_kv).astype(BF16)
    ones = jnp.ones((MLA_VT_ROWS - MLA_V_DIM, v_t.shape[1]), BF16)
    for h in range(N_MLA_HEADS):
        vt_ref[h * MLA_VT_ROWS:h * MLA_VT_ROWS + MLA_V_DIM, :] = v_t[h * MLA_V_DIM:(h + 1) * MLA_V_DIM, :]
        vt_ref[h * MLA_VT_ROWS + MLA_V_DIM:(h + 1) * MLA_VT_ROWS, :] = ones
    omem_ref[...] = _memory_attention(_dot(xb, wqm_ref[...]), km_ref, vm_ref).astype(BF16)


def _mla_inproj(x, w, cos, sin, km, vm):
    b, s, d = x.shape
    m = km.shape[1]
    tm = min(PROJ_ROW_TILE, s)
    hw = N_MLA_HEADS * LANES
    mem = pl.BlockSpec((None, m, MEM_WIDTH), lambda bi, i: (bi, 0, 0))
    table = pl.BlockSpec((tm, LANES), lambda bi, i: (i, 0))
    table_t = pl.BlockSpec((LANES, tm), lambda bi, i: (0, i))
    rows = lambda width: pl.BlockSpec((None, tm, width), lambda bi, i: (bi, i, 0))
    cols = lambda width: pl.BlockSpec((None, width, tm), lambda bi, i: (bi, 0, i))
    weights = [w["cq"], w["ckv"], w["kr"], w["kr_rot"], w["qmem"], w["gq"], w["gkv"],
               w["q"].T, w["k"], w["v"].T]
    return pl.pallas_call(
        _mla_inproj_kernel,
        grid=(b, s // tm),
        in_specs=([rows(d)] + [_resident(a.shape) for a in weights]
                  + [table, table, table_t, table_t, mem, mem]),
        out_specs=[cols(hw), rows(hw), cols(N_MLA_HEADS * MLA_VT_ROWS), rows(MEM_WIDTH)],
        out_shape=[jax.ShapeDtypeStruct((b, hw, s), BF16), jax.ShapeDtypeStruct((b, s, hw), BF16),
                   jax.ShapeDtypeStruct((b, N_MLA_HEADS * MLA_VT_ROWS, s), BF16),
                   jax.ShapeDtypeStruct((b, s, MEM_WIDTH), BF16)],
        compiler_params=_params("parallel", "parallel"),
        name="mla_inproj",
    )(x, *weights, cos, sin, cos.T, sin.T, km, vm)


def _mla_attn_kernel(qt_ref, k_ref, vt_ref, o_ref, s0_ref, s1_ref, b0_ref, b1_ref, m_ref, acc_ref):
    t = qt_ref.shape[1]
    i = pl.program_id(2)
    m_ref[...] = jnp.full_like(m_ref, -jnp.inf)
    acc_ref[...] = jnp.zeros_like(acc_ref)

    def scores(j, s_ref, b_ref):
        start = pl.multiple_of(j * t, t)
        for h in range(2):
            tile = slice(h * LANES, (h + 1) * LANES)
            s_t = _dot(k_ref[pl.ds(start, t), tile], qt_ref[tile, :])
            s_ref[h] = s_t
            b_ref[h] = jnp.max(s_t, axis=0, keepdims=True)

    def update(j, s_ref, b_ref, masked):
        start = pl.multiple_of(j * t, t)
        for h in range(2):
            s_t = s_ref[h]
            if masked:
                key = lax.broadcasted_iota(jnp.int32, (t, t), 0)
                qry = lax.broadcasted_iota(jnp.int32, (t, t), 1)
                s_t = jnp.where(key <= qry, s_t, -jnp.inf)
                block_max = jnp.max(s_t, axis=0, keepdims=True)
            else:
                block_max = b_ref[h]
            m_old = m_ref[h]
            m_new = jnp.maximum(m_old, block_max)
            alpha = jnp.exp2(m_old - m_new)
            prob_t = jnp.exp2(s_t - m_new)
            v_t = vt_ref[h * MLA_VT_ROWS:(h + 1) * MLA_VT_ROWS, pl.ds(start, t)]
            acc_ref[h] = alpha * acc_ref[h] + _dot(v_t, prob_t.astype(BF16))
            m_ref[h] = m_new

    def two_full_steps(j):
        scores(j + 1, s1_ref, b1_ref)
        update(j, s0_ref, b0_ref, masked=False)
        scores(j + 2, s0_ref, b0_ref)
        update(j + 1, s1_ref, b1_ref, masked=False)

    def eight_full_steps(jj, carry):
        for step in range(0, MLA_UNROLL, 2):
            two_full_steps(MLA_UNROLL * jj + step)
        return carry

    scores(0, s0_ref, b0_ref)
    lax.fori_loop(0, i // MLA_UNROLL, eight_full_steps, 0)
    span = MLA_UNROLL // 2
    while span >= 2:
        @pl.when(i % (2 * span) >= span)
        def _more_full_steps(span=span):
            for step in range(0, span, 2):
                two_full_steps((2 * span) * (i // (2 * span)) + step)
        span //= 2

    @pl.when(i % 2 == 0)
    def _diagonal_in_s0():
        update(i, s0_ref, b0_ref, masked=True)

    @pl.when(i % 2 == 1)
    def _last_full_then_diagonal():
        scores(i, s1_ref, b1_ref)
        update(i - 1, s0_ref, b0_ref, masked=False)
        update(i, s1_ref, b1_ref, masked=True)

    heads = [acc_ref[h, :MLA_V_DIM, :] / acc_ref[h, MLA_V_DIM:MLA_V_DIM + 1, :] for h in range(2)]
    o_ref[...] = jnp.concatenate(heads, axis=0).astype(o_ref.dtype)


def _mla_attention(qt, k, vt):
    b, s, _ = k.shape
    t = min(MLA_TILE, s)
    return pl.pallas_call(
        _mla_attn_kernel,
        grid=(b, N_MLA_HEADS // 2, s // t),
        in_specs=[pl.BlockSpec((None, 2 * LANES, t), lambda bi, p, i: (bi, p, i)),
                  pl.BlockSpec((None, s, 2 * LANES), lambda bi, p, i: (bi, 0, p)),
                  pl.BlockSpec((None, 2 * MLA_VT_ROWS, s), lambda bi, p, i: (bi, p, 0))],
        out_specs=pl.BlockSpec((None, LANES, t), lambda bi, p, i: (bi, p, i)),
        out_shape=jax.ShapeDtypeStruct((b, MLA_WIDTH, s), BF16),
        scratch_shapes=[pltpu.VMEM((2, t, t), F32), pltpu.VMEM((2, t, t), F32),
                        pltpu.VMEM((2, 1, t), F32), pltpu.VMEM((2, 1, t), F32),
                        pltpu.VMEM((2, 1, t), F32), pltpu.VMEM((2, MLA_VT_ROWS, t), F32)],
        compiler_params=_params("parallel", "parallel", "arbitrary"),
        name="mla_attention",
    )(qt, k, vt)


def _outproj_ffn_kernel(x_ref, a_ref, m_ref, wo_ref, gm_ref, bm_ref, win32_ref, wout32_ref, g_ref, b_ref,
                        o_ref, win_ref, wout_ref):
    step = pl.program_id(0)

    @pl.when(step < FFN_CAST_STEPS)
    def _cast():
        _cast_ffn_slab(step, win32_ref, wout32_ref, win_ref, wout_ref)

    @pl.when(step >= FFN_CAST_STEPS)
    def _compute():
        width = a_ref.shape[0]
        heads = lax.dot_general(a_ref[...], wo_ref[:width, :], (((0,), (0,)), ((), ())),
                                preferred_element_type=F32)
        mixed = heads + _dot(m_ref[...], wo_ref[width:, :])
        x = _layer_norm(DEEPNORM_ALPHA * x_ref[...] + mixed, gm_ref[...], bm_ref[...])
        o_ref[...] = _ffn_block(x, win_ref, wout_ref, g_ref, b_ref)


def _outproj_ffn(x, attn_t, omem, w_out, g_mix, b_mix, w_in, w_ffn_out, layer, g, b):
    t, d = x.shape
    tm = min(ROW_TILE, t)
    width, s = attn_t.shape[1:]
    per_batch = s // tm
    assert per_batch * tm == s
    vec = lambda a: a.reshape(1, d)

    def attn_index(i):
        tile = jnp.maximum(i - FFN_CAST_STEPS, 0)
        return tile // per_batch, 0, tile % per_batch

    return pl.pallas_call(
        _outproj_ffn_kernel,
        grid=(FFN_CAST_STEPS + t // tm,),
        in_specs=([_row_tile_spec(tm, d), pl.BlockSpec((None, width, tm), attn_index),
                   _row_tile_spec(tm, omem.shape[1]),
                   _resident(w_out.shape), _resident((1, d)), _resident((1, d))]
                  + _ffn_weight_specs(layer, d) + [_resident((1, d)), _resident((1, d))]),
        out_specs=_row_tile_spec(tm, d),
        out_shape=jax.ShapeDtypeStruct((t, d), F32),
        scratch_shapes=_ffn_weight_scratch(d),
        compiler_params=_params("arbitrary"),
        name="outproj_ffn",
    )(x, attn_t, omem, w_out, vec(g_mix), vec(b_mix), w_in, w_ffn_out, vec(g), vec(b))


def _rotate_half_columns(w):
    half = w.shape[1] // 2
    return jnp.concatenate([-w[:, half:], w[:, :half]], axis=1)


def _head_tiles(cols_per_head, offset):
    k, h, c = cols_per_head.shape
    tiles = jnp.zeros((k, h, LANES), cols_per_head.dtype).at[:, :, offset:offset + c].set(cols_per_head)
    return tiles.reshape(k, h * LANES)


def _mla_weights(w_in, q_norm_g, w_uq, kv_norm_g, w_ukv):
    q_dim = MLA_NOPE_DIM + MLA_ROPE_DIM
    kr0 = MLA_Q_RANK + MLA_KV_RANK
    w_kr = w_in[:, kr0:kr0 + MLA_ROPE_DIM]
    uq = w_uq.reshape(MLA_Q_RANK, N_MLA_HEADS, q_dim)
    ukv = w_ukv.reshape(MLA_KV_RANK, N_MLA_HEADS, MLA_NOPE_DIM + MLA_V_DIM)
    as_tile = lambda w, off: _head_tiles(w[:, None, :], off)
    w = {
        "cq": w_in[:, :MLA_Q_RANK],
        "ckv": w_in[:, MLA_Q_RANK:kr0],
        "kr": as_tile(w_kr, MLA_NOPE_DIM),
        "kr_rot": as_tile(_rotate_half_columns(w_kr), MLA_NOPE_DIM),
        "qmem": w_in[:, kr0 + MLA_ROPE_DIM:],
        "q": _head_tiles(uq, 0),
        "k": _head_tiles(ukv[:, :, :MLA_NOPE_DIM], 0),
        "v": ukv[:, :, MLA_NOPE_DIM:].reshape(MLA_KV_RANK, MLA_WIDTH),
    }
    w = {name: a.astype(BF16) for name, a in w.items()}
    w["gq"] = q_norm_g.reshape(1, MLA_Q_RANK)
    w["gkv"] = kv_norm_g.reshape(1, MLA_KV_RANK)
    return w


def _rope_tables(s):
    freqs = ROPE_BASE ** (-jnp.arange(0, MLA_ROPE_DIM, 2, dtype=F32) / MLA_ROPE_DIM)
    ang = jnp.arange(s).astype(F32)[:, None] * freqs[None, :]
    pad = jnp.zeros((s, LANES - MLA_NOPE_DIM - MLA_ROPE_DIM), F32)
    cos = jnp.concatenate([jnp.ones((s, MLA_NOPE_DIM), F32), jnp.cos(ang), jnp.cos(ang), pad], axis=1)
    sin = jnp.concatenate([jnp.zeros((s, MLA_NOPE_DIM), F32), jnp.sin(ang), jnp.sin(ang), pad], axis=1)
    return cos, sin


def kernel(x, mem, ln_ffn1_g, ln_ffn1_b, ln_mix_g, ln_mix_b, ln_ffn2_g, ln_ffn2_b, ffn1_w_in, ffn1_w_out, ffn2_w_in, ffn2_w_out, sb_w_in, mla_w_in, mla_q_norm_g, mla_w_uq, mla_kv_norm_g, mla_w_ukv, mem_w_kv, w_out):
    b, s, d = x.shape
    flat = lambda a: a.reshape(b * s, a.shape[-1])
    cos, sin = _rope_tables(s)
    for i in range(DEPTH):
        x = _ffn_ln(flat(x), ffn1_w_in, ffn1_w_out, i, ln_ffn1_g[i], ln_ffn1_b[i]).reshape(b, s, d)
        km, vm = _mem_kv(mem, mem_w_kv[i].astype(BF16))
        j = i // 2
        if i % 2 == 0:
            qt, k, vt, omem = _sb_inproj(x, sb_w_in[j].astype(BF16), km, vm)
            attn = _sb_attention(qt, k, vt)
        else:
            w = _mla_weights(mla_w_in[j], mla_q_norm_g[j], mla_w_uq[j], mla_kv_norm_g[j], mla_w_ukv[j])
            qt, k, vt, omem = _mla_inproj(x, w, cos, sin, km, vm)
            attn = _mla_attention(qt, k, vt)
        x = _outproj_ffn(flat(x), attn, flat(omem), w_out[i].astype(BF16), ln_mix_g[i], ln_mix_b[i],
                         ffn2_w_in, ffn2_w_out, i, ln_ffn2_g[i], ln_ffn2_b[i]).reshape(b, s, d)
    return x
```

```python
import math

import jax
import jax.numpy as jnp
from jax import lax
from jax.experimental import pallas as pl
from jax.experimental.pallas import tpu as pltpu

D_MODEL = 1024
DEPTH = 2
HEAD_DIM = 64
N_SB_HEADS = 12
N_MLA_HEADS = 12
MLA_NOPE_DIM = 64
MLA_ROPE_DIM = 32
MLA_V_DIM = 64
MLA_Q_RANK = 384
MLA_KV_RANK = 256
ROPE_BASE = 10000.0
N_MEM_HEADS = 4
MEM_HEAD_DIM = 64
MEM_WIDTH = N_MEM_HEADS * MEM_HEAD_DIM
SB_WIDTH = N_SB_HEADS * HEAD_DIM
MLA_WIDTH = N_MLA_HEADS * MLA_V_DIM
D_FF = 2816
LN_EPS = 1e-5
RMS_EPS = 1e-6
DEEPNORM_ALPHA = (2 * DEPTH) ** 0.25

LANES = 128
VMEM_LIMIT_BYTES = 56 * 1024 * 1024

ROW_TILE = 512
PROJ_ROW_TILE = 1024
MXU_TILE = 256
FFN_SPLIT = (0, 6 * MXU_TILE, D_FF)
FFN_CAST_STEPS = D_FF // MXU_TILE
BF16_ROWS = 16
MLA_VT_ROWS = MLA_V_DIM + BF16_ROWS
SB_QUERY_TILE = 2048
SB_SUB = LANES
SB_WINDOW = 2 * SB_SUB
MLA_TILE = 512
MLA_UNROLL = 16
SB_LOG_WEIGHT_FLOOR = -88.0
MLA_LOG2_SCALE = math.log2(math.e) / math.sqrt(MLA_NOPE_DIM + MLA_ROPE_DIM)

BF16 = jnp.bfloat16
F32 = jnp.float32


def _dot(a, b):
    return jnp.dot(a, b, preferred_element_type=F32)


def _dot_nt(a, b):
    return lax.dot_general(a, b, (((1,), (1,)), ((), ())), preferred_element_type=F32)


def _layer_norm(y, g, b):
    mu = jnp.mean(y, axis=-1, keepdims=True)
    d = y - mu
    var = jnp.mean(d * d, axis=-1, keepdims=True)
    return d * lax.rsqrt(var + LN_EPS) * g + b


def _rms_norm(x, g):
    return x * lax.rsqrt(jnp.mean(x * x, axis=-1, keepdims=True) + RMS_EPS) * g


def _first_head(shape, axis):
    return lax.broadcasted_iota(jnp.int32, shape, axis) < HEAD_DIM


def _params(*semantics):
    return pltpu.CompilerParams(dimension_semantics=semantics, vmem_limit_bytes=VMEM_LIMIT_BYTES)


def _resident(shape):
    nd = len(shape)
    return pl.BlockSpec(shape, lambda *_: (0,) * nd, pipeline_mode=pl.Buffered(1))


def _ffn_block(x, win_ref, wout_ref, g_ref, b_ref):
    xb = x.astype(BF16)
    acc = None
    for lo, hi in zip(FFN_SPLIT[:-1], FFN_SPLIT[1:]):
        gate = _dot(xb, win_ref[:, lo:hi])
        up = _dot(xb, win_ref[:, D_FF + lo:D_FF + hi])
        act = (gate * jax.nn.sigmoid(gate) * up).astype(BF16)
        part = _dot(act, wout_ref[lo:hi, :])
        acc = part if acc is None else acc + part
    return _layer_norm(DEEPNORM_ALPHA * x + acc, g_ref[...], b_ref[...])


def _cast_ffn_slab(step, win32_ref, wout32_ref, win_ref, wout_ref):
    cols, rows = win32_ref.shape[1], wout32_ref.shape[0]
    win_ref[:, pl.ds(pl.multiple_of(step * cols, cols), cols)] = win32_ref[...].astype(BF16)
    wout_ref[pl.ds(pl.multiple_of(step * rows, rows), rows), :] = (wout32_ref[...] * 0.5).astype(BF16)


def _ffn_weight_specs(layer, d):
    last = FFN_CAST_STEPS - 1
    return [pl.BlockSpec((None, d, 2 * D_FF // FFN_CAST_STEPS), lambda i: (layer, 0, jnp.minimum(i, last))),
            pl.BlockSpec((None, D_FF // FFN_CAST_STEPS, d), lambda i: (layer, jnp.minimum(i, last), 0))]


def _ffn_weight_scratch(d):
    return [pltpu.VMEM((d, 2 * D_FF), BF16), pltpu.VMEM((D_FF, d), BF16)]


def _row_tile_spec(tm, width):
    return pl.BlockSpec((tm, width), lambda i: (jnp.maximum(i - FFN_CAST_STEPS, 0), 0))


def _ffn_ln_kernel(x_ref, win32_ref, wout32_ref, g_ref, b_ref, o_ref, win_ref, wout_ref):
    step = pl.program_id(0)

    @pl.when(step < FFN_CAST_STEPS)
    def _cast():
        _cast_ffn_slab(step, win32_ref, wout32_ref, win_ref, wout_ref)

    @pl.when(step >= FFN_CAST_STEPS)
    def _compute():
        o_ref[...] = _ffn_block(x_ref[...], win_ref, wout_ref, g_ref, b_ref)


def _ffn_ln(x, w_in, w_out, layer, g, b):
    t, d = x.shape
    tm = min(ROW_TILE, t)
    return pl.pallas_call(
        _ffn_ln_kernel,
        grid=(FFN_CAST_STEPS + t // tm,),
        in_specs=[_row_tile_spec(tm, d)] + _ffn_weight_specs(layer, d) + [_resident((1, d)), _resident((1, d))],
        out_specs=_row_tile_spec(tm, d),
        out_shape=jax.ShapeDtypeStruct((t, d), F32),
        scratch_shapes=_ffn_weight_scratch(d),
        compiler_params=_params("arbitrary"),
        name="ffn_ln",
    )(x, w_in, w_out, g.reshape(1, d), b.reshape(1, d))


def _mem_kv_kernel(mem_ref, w_ref, k_ref, v_ref):
    kv = _dot(mem_ref[...].astype(BF16), w_ref[...])
    k_ref[...] = kv[:, :MEM_WIDTH].astype(BF16)
    v_ref[...] = kv[:, MEM_WIDTH:].astype(BF16)


def _mem_kv(mem, w_kv):
    b, m, d = mem.shape
    out = pl.BlockSpec((None, m, MEM_WIDTH), lambda i: (i, 0, 0))
    return pl.pallas_call(
        _mem_kv_kernel,
        grid=(b,),
        in_specs=[pl.BlockSpec((None, m, d), lambda i: (i, 0, 0)), _resident(w_kv.shape)],
        out_specs=[out, out],
        out_shape=[jax.ShapeDtypeStruct((b, m, MEM_WIDTH), BF16)] * 2,
        compiler_params=_params("parallel"),
        name="mem_kv",
    )(mem, w_kv)


def _memory_attention(q_mem, km_ref, vm_ref):
    tm = q_mem.shape[0]
    low = _first_head((tm, LANES), 1)
    scale = 1.0 / math.sqrt(MEM_HEAD_DIM)
    outs = []
    for p in range(MEM_WIDTH // LANES):
        q2 = q_mem[:, p * LANES:(p + 1) * LANES]
        k2 = km_ref[:, p * LANES:(p + 1) * LANES]
        v2 = vm_ref[:, p * LANES:(p + 1) * LANES]
        pair = None
        for own in (low, jnp.logical_not(low)):
            qh = jnp.where(own, q2, 0.0).astype(BF16)
            s = _dot_nt(qh, k2) * scale
            e = jnp.exp(s - jnp.max(s, axis=-1, keepdims=True))
            prob = e / jnp.sum(e, axis=-1, keepdims=True)
            o = _dot(prob.astype(BF16), v2)
            pair = o if pair is None else jnp.where(low, pair, o)
        outs.append(pair)
    return jnp.concatenate(outs, axis=-1)


def _sb_inproj_kernel(x_ref, wqt_ref, wk_ref, wvt_ref, wqm_ref, km_ref, vm_ref,
                      qt_ref, k_ref, vt_ref, omem_ref):
    xb = x_ref[...].astype(BF16)
    qt_ref[...] = (_dot_nt(wqt_ref[...], xb) * (1.0 / math.sqrt(HEAD_DIM))).astype(BF16)
    k_ref[...] = _dot(xb, wk_ref[...]).astype(BF16)
    vt_ref[...] = _dot_nt(wvt_ref[...], xb).astype(BF16)
    omem_ref[...] = _memory_attention(_dot(xb, wqm_ref[...]), km_ref, vm_ref).astype(BF16)


def _sb_inproj(x, w_in, km, vm):
    b, s, d = x.shape
    m = km.shape[1]
    tm = min(PROJ_ROW_TILE, s)
    w = SB_WIDTH
    weights = [w_in[:, :w].T, w_in[:, w:2 * w], w_in[:, 2 * w:3 * w].T, w_in[:, 3 * w:]]
    mem = pl.BlockSpec((None, m, MEM_WIDTH), lambda bi, i: (bi, 0, 0))
    rows = lambda width: pl.BlockSpec((None, tm, width), lambda bi, i: (bi, i, 0))
    cols = pl.BlockSpec((None, w, tm), lambda bi, i: (bi, 0, i))
    return pl.pallas_call(
        _sb_inproj_kernel,
        grid=(b, s // tm),
        in_specs=[rows(d)] + [_resident(a.shape) for a in weights] + [mem, mem],
        out_specs=[cols, rows(w), cols, rows(MEM_WIDTH)],
        out_shape=[jax.ShapeDtypeStruct((b, w, s), BF16), jax.ShapeDtypeStruct((b, s, w), BF16),
                   jax.ShapeDtypeStruct((b, w, s), BF16), jax.ShapeDtypeStruct((b, s, MEM_WIDTH), BF16)],
        compiler_params=_params("parallel", "parallel"),
        name="sb_inproj",
    )(x, *weights, km, vm)


def _split3(x):
    hi = x.astype(BF16)
    r = x - hi.astype(F32)
    mid = r.astype(BF16)
    lo = (r - mid.astype(F32)).astype(BF16)
    return hi, mid, lo


def _later3(n):
    row = lax.broadcasted_iota(jnp.int32, (n, n), 0)
    col = lax.broadcasted_iota(jnp.int32, (n, n), 1)
    later = jnp.where(col > row, 1.0, 0.0).astype(BF16)
    return jnp.concatenate([later, later, later], axis=1)


def _sb_logs(z, visible):
    decay = jnp.log(1.0 + jnp.exp2(jnp.abs(z) * -math.log2(math.e))) + jnp.maximum(z, 0.0)
    return z - decay, jnp.where(visible, decay, 0.0)


def _sb_behind(later3, decay):
    return _dot(later3, jnp.concatenate(_split3(decay), axis=0))


def _sb_chain(qt, k_ref, vt_ref, start, n_keys, first_query, tail_in):
    key = lax.broadcasted_iota(jnp.int32, (n_keys, SB_SUB), 0)
    qry = lax.broadcasted_iota(jnp.int32, (n_keys, SB_SUB), 1)
    z = _dot(k_ref[pl.ds(start, n_keys), :], qt)
    visible = (key - qry) < (first_query - start)
    log_beta, decay = _sb_logs(z, visible)
    behind = _sb_behind(_later3(n_keys), decay)
    w = jnp.where(visible, jnp.exp(log_beta - behind + tail_in), 0.0)
    out_t = _dot(vt_ref[:, pl.ds(start, n_keys)], w.astype(BF16))
    return out_t, tail_in - jnp.sum(decay, axis=0, keepdims=True)


def _sb_attn_kernel(qt_ref, k_ref, vt_ref, o_ref, acc_ref, tail_ref):
    tq = qt_ref.shape[1]
    n_sub = tq // SB_SUB
    q0 = pl.program_id(2) * tq
    first = _first_head((LANES, SB_SUB), 0)
    owners = (first, jnp.logical_not(first))
    subs = [slice(u * SB_SUB, (u + 1) * SB_SUB) for u in range(n_sub)]

    def head_queries(u, h):
        return jnp.where(owners[h], qt_ref[:, subs[u]].astype(F32), 0.0).astype(BF16)

    starts = [pl.multiple_of(jnp.maximum(q0 + (u - 1) * SB_SUB, 0), SB_SUB) for u in range(n_sub)]
    later3 = _later3(SB_WINDOW)
    key = lax.broadcasted_iota(jnp.int32, (SB_WINDOW, 2 * SB_SUB), 0)
    qry = lax.broadcasted_iota(jnp.int32, (SB_WINDOW, 2 * SB_SUB), 1) & (SB_SUB - 1)
    ahead = key - qry
    zs = [_dot(k_ref[pl.ds(starts[u], SB_WINDOW), :],
               jnp.concatenate([head_queries(u, 0), head_queries(u, 1)], axis=1)) for u in range(n_sub)]
    stage = []
    for u in range(n_sub):
        visible = ahead < (q0 + u * SB_SUB - starts[u])
        log_beta, decay = _sb_logs(zs[u], visible)
        stage.append((visible, log_beta, decay, _sb_behind(later3, decay)))
    tail_max = {}
    pending = jnp.float32(-jnp.inf)
    for u in range(n_sub):
        visible, log_beta, decay, behind = stage[u]
        w = jnp.where(visible, jnp.exp(log_beta - behind), 0.0)
        out_t = _dot(vt_ref[:, pl.ds(starts[u], SB_WINDOW)], w.astype(BF16))
        tail = -jnp.sum(decay, axis=0, keepdims=True)
        for h in range(2):
            acc_ref[h, :, subs[u]] = out_t[:, h * SB_SUB:(h + 1) * SB_SUB]
            tail_ref[h, :, subs[u]] = tail[:, h * SB_SUB:(h + 1) * SB_SUB]
            tail_max[u, h] = jnp.max(tail[:, h * SB_SUB:(h + 1) * SB_SUB])
            pending = jnp.maximum(pending, jnp.where(starts[u] > 0, tail_max[u, h], -jnp.inf))

    @pl.when(pending > SB_LOG_WEIGHT_FLOOR)
    def _walk_further_back():
        for u in range(n_sub):
            for h in range(2):
                qt = head_queries(u, h)

                def cond(state):
                    start, tail_max = state
                    return jnp.logical_and(start > 0, tail_max > SB_LOG_WEIGHT_FLOOR)

                def body(state, qt=qt, u=u, h=h):
                    start = pl.multiple_of(state[0] - SB_SUB, SB_SUB)
                    out_t, tail = _sb_chain(qt, k_ref, vt_ref, start, SB_SUB,
                                            q0 + u * SB_SUB, tail_ref[h, :, subs[u]])
                    acc_ref[h, :, subs[u]] += out_t
                    tail_ref[h, :, subs[u]] = tail
                    return start, jnp.max(tail)

                lax.while_loop(cond, body, (starts[u], tail_max[u, h]))

    out_t = jnp.where(_first_head((LANES, tq), 0), acc_ref[0], acc_ref[1])
    o_ref[...] = out_t.astype(o_ref.dtype)


def _sb_attention(qt, k, vt):
    b, s, _ = k.shape
    tq = min(SB_QUERY_TILE, s)
    pairs = SB_WIDTH // LANES
    return pl.pallas_call(
        _sb_attn_kernel,
        grid=(b, pairs, s // tq),
        in_specs=[pl.BlockSpec((None, LANES, tq), lambda bi, p, i: (bi, p, i)),
                  pl.BlockSpec((None, s, LANES), lambda bi, p, i: (bi, 0, p)),
                  pl.BlockSpec((None, LANES, s), lambda bi, p, i: (bi, p, 0))],
        out_specs=pl.BlockSpec((None, LANES, tq), lambda bi, p, i: (bi, p, i)),
        out_shape=jax.ShapeDtypeStruct((b, SB_WIDTH, s), BF16),
        scratch_shapes=[pltpu.VMEM((2, LANES, tq), F32), pltpu.VMEM((2, 1, tq), F32)],
        compiler_params=_params("parallel", "parallel", "arbitrary"),
        name="sb_attention",
    )(qt, k, vt)


def _mla_inproj_kernel(x_ref, wcq_ref, wckv_ref, wkr_ref, wkrr_ref, wqm_ref, gq_ref, gkv_ref,
                       wqt_ref, wk_ref, wvt_ref, cos_ref, sin_ref, cost_ref, sint_ref,
                       km_ref, vm_ref, qt_ref, k_ref, vt_ref, omem_ref):
    xb = x_ref[...].astype(BF16)
    cos, sin = cos_ref[...], sin_ref[...]
    cos_t, sin_t = cost_ref[...], sint_ref[...]
    c_q = _rms_norm(_dot(xb, wcq_ref[...]), gq_ref[...]).astype(BF16)
    c_kv = _rms_norm(_dot(xb, wckv_ref[...]), gkv_ref[...]).astype(BF16)
    k_rope = _dot(xb, wkr_ref[...]) * cos + _dot(xb, wkrr_ref[...]) * sin
    k_rope2 = jnp.concatenate([k_rope, k_rope], axis=-1)
    r0, r1, r2 = MLA_NOPE_DIM, MLA_NOPE_DIM + MLA_ROPE_DIM // 2, MLA_NOPE_DIM + MLA_ROPE_DIM
    for p in range(N_MLA_HEADS // 2):
        lo, hi = 2 * p * LANES, 2 * (p + 1) * LANES
        q_pair = _dot_nt(wqt_ref[lo:hi, :], c_q)
        for h in range(2):
            q_t = q_pair[h * LANES:(h + 1) * LANES]
            rotated = jnp.concatenate([q_t[:r0], -q_t[r1:r2], q_t[r0:r1], q_t[r2:]], axis=0)
            q_t = (q_t * cos_t + rotated * sin_t) * MLA_LOG2_SCALE
            qt_ref[lo + h * LANES:lo + (h + 1) * LANES, :] = q_t.astype(BF16)
        k_ref[:, lo:hi] = (_dot(c_kv, wk_ref[:, lo:hi]) + k_rope2).astype(BF16)
    v_t = _dot_nt(wvt_ref[...], c_kv).astype(BF16)
    ones = jnp.ones((MLA_VT_ROWS - MLA_V_DIM, v_t.shape[1]), BF16)
    for h in range(N_MLA_HEADS):
        vt_ref[h * MLA_VT_ROWS:h * MLA_VT_ROWS + MLA_V_DIM, :] = v_t[h * MLA_V_DIM:(h + 1) * MLA_V_DIM, :]
        vt_ref[h * MLA_VT_ROWS + MLA_V_DIM:(h + 1) * MLA_VT_ROWS, :] = ones
    omem_ref[...] = _memory_attention(_dot(xb, wqm_ref[...]), km_ref, vm_ref).astype(BF16)


def _mla_inproj(x, w, cos, sin, km, vm):
    b, s, d = x.shape
    m = km.shape[1]
    tm = min(PROJ_ROW_TILE, s)
    hw = N_MLA_HEADS * LANES
    mem = pl.BlockSpec((None, m, MEM_WIDTH), lambda bi, i: (bi, 0, 0))
    table = pl.BlockSpec((tm, LANES), lambda bi, i: (i, 0))
    table_t = pl.BlockSpec((LANES, tm), lambda bi, i: (0, i))
    rows = lambda width: pl.BlockSpec((None, tm, width), lambda bi, i: (bi, i, 0))
    cols = lambda width: pl.BlockSpec((None, width, tm), lambda bi, i: (bi, 0, i))
    weights = [w["cq"], w["ckv"], w["kr"], w["kr_rot"], w["qmem"], w["gq"], w["gkv"],
               w["q"].T, w["k"], w["v"].T]
    return pl.pallas_call(
        _mla_inproj_kernel,
        grid=(b, s // tm),
        in_specs=([rows(d)] + [_resident(a.shape) for a in weights]
                  + [table, table, table_t, table_t, mem, mem]),
        out_specs=[cols(hw), rows(hw), cols(N_MLA_HEADS * MLA_VT_ROWS), rows(MEM_WIDTH)],
        out_shape=[jax.ShapeDtypeStruct((b, hw, s), BF16), jax.ShapeDtypeStruct((b, s, hw), BF16),
                   jax.ShapeDtypeStruct((b, N_MLA_HEADS * MLA_VT_ROWS, s), BF16),
                   jax.ShapeDtypeStruct((b, s, MEM_WIDTH), BF16)],
        compiler_params=_params("parallel", "parallel"),
        name="mla_inproj",
    )(x, *weights, cos, sin, cos.T, sin.T, km, vm)


def _mla_attn_kernel(qt_ref, k_ref, vt_ref, o_ref, s0_ref, s1_ref, b0_ref, b1_ref, m_ref, acc_ref):
    t = qt_ref.shape[1]
    i = pl.program_id(2)
    m_ref[...] = jnp.full_like(m_ref, -jnp.inf)
    acc_ref[...] = jnp.zeros_like(acc_ref)

    def scores(j, s_ref, b_ref):
        start = pl.multiple_of(j * t, t)
        for h in range(2):
            tile = slice(h * LANES, (h + 1) * LANES)
            s_t = _dot(k_ref[pl.ds(start, t), tile], qt_ref[tile, :])
            s_ref[h] = s_t
            b_ref[h] = jnp.max(s_t, axis=0, keepdims=True)

    def update(j, s_ref, b_ref, masked):
        start = pl.multiple_of(j * t, t)
        for h in range(2):
            s_t = s_ref[h]
            if masked:
                key = lax.broadcasted_iota(jnp.int32, (t, t), 0)
                qry = lax.broadcasted_iota(jnp.int32, (t, t), 1)
                s_t = jnp.where(key <= qry, s_t, -jnp.inf)
                block_max = jnp.max(s_t, axis=0, keepdims=True)
            else:
                block_max = b_ref[h]
            m_old = m_ref[h]
            m_new = jnp.maximum(m_old, block_max)
            alpha = jnp.exp2(m_old - m_new)
            prob_t = jnp.exp2(s_t - m_new)
            v_t = vt_ref[h * MLA_VT_ROWS:(h + 1) * MLA_VT_ROWS, pl.ds(start, t)]
            acc_ref[h] = alpha * acc_ref[h] + _dot(v_t, prob_t.astype(BF16))
            m_ref[h] = m_new

    def two_full_steps(j):
        scores(j + 1, s1_ref, b1_ref)
        update(j, s0_ref, b0_ref, masked=False)
        scores(j + 2, s0_ref, b0_ref)
        update(j + 1, s1_ref, b1_ref, masked=False)

    def eight_full_steps(jj, carry):
        for step in range(0, MLA_UNROLL, 2):
            two_full_steps(MLA_UNROLL * jj + step)
        return carry

    scores(0, s0_ref, b0_ref)
    lax.fori_loop(0, i // MLA_UNROLL, eight_full_steps, 0)
    span = MLA_UNROLL // 2
    while span >= 2:
        @pl.when(i % (2 * span) >= span)
        def _more_full_steps(span=span):
            for step in range(0, span, 2):
                two_full_steps((2 * span) * (i // (2 * span)) + step)
        span //= 2

    @pl.when(i % 2 == 0)
    def _diagonal_in_s0():
        update(i, s0_ref, b0_ref, masked=True)

    @pl.when(i % 2 == 1)
    def _last_full_then_diagonal():
        scores(i, s1_ref, b1_ref)
        update(i - 1, s0_ref, b0_ref, masked=False)
        update(i, s1_ref, b1_ref, masked=True)

    heads = [acc_ref[h, :MLA_V_DIM, :] / acc_ref[h, MLA_V_DIM:MLA_V_DIM + 1, :] for h in range(2)]
    o_ref[...] = jnp.concatenate(heads, axis=0).astype(o_ref.dtype)


def _mla_attention(qt, k, vt):
    b, s, _ = k.shape
    t = min(MLA_TILE, s)
    return pl.pallas_call(
        _mla_attn_kernel,
        grid=(b, N_MLA_HEADS // 2, s // t),
        in_specs=[pl.BlockSpec((None, 2 * LANES, t), lambda bi, p, i: (bi, p, i)),
                  pl.BlockSpec((None, s, 2 * LANES), lambda bi, p, i: (bi, 0, p)),
                  pl.BlockSpec((None, 2 * MLA_VT_ROWS, s), lambda bi, p, i: (bi, p, 0))],
        out_specs=pl.BlockSpec((None, LANES, t), lambda bi, p, i: (bi, p, i)),
        out_shape=jax.ShapeDtypeStruct((b, MLA_WIDTH, s), BF16),
        scratch_shapes=[pltpu.VMEM((2, t, t), F32), pltpu.VMEM((2, t, t), F32),
                        pltpu.VMEM((2, 1, t), F32), pltpu.VMEM((2, 1, t), F32),
                        pltpu.VMEM((2, 1, t), F32), pltpu.VMEM((2, MLA_VT_ROWS, t), F32)],
        compiler_params=_params("parallel", "parallel", "arbitrary"),
        name="mla_attention",
    )(qt, k, vt)


def _outproj_ffn_kernel(x_ref, a_ref, m_ref, wo_ref, gm_ref, bm_ref, win32_ref, wout32_ref, g_ref, b_ref,
                        o_ref, win_ref, wout_ref):
    step = pl.program_id(0)

    @pl.when(step < FFN_CAST_STEPS)
    def _cast():
        _cast_ffn_slab(step, win32_ref, wout32_ref, win_ref, wout_ref)

    @pl.when(step >= FFN_CAST_STEPS)
    def _compute():
        width = a_ref.shape[0]
        heads = lax.dot_general(a_ref[...], wo_ref[:width, :], (((0,), (0,)), ((), ())),
                                preferred_element_type=F32)
        mixed = heads + _dot(m_ref[...], wo_ref[width:, :])
        x = _layer_norm(DEEPNORM_ALPHA * x_ref[...] + mixed, gm_ref[...], bm_ref[...])
        o_ref[...] = _ffn_block(x, win_ref, wout_ref, g_ref, b_ref)


def _outproj_ffn(x, attn_t, omem, w_out, g_mix, b_mix, w_in, w_ffn_out, layer, g, b):
    t, d = x.shape
    tm = min(ROW_TILE, t)
    width, s = attn_t.shape[1:]
    per_batch = s // tm
    assert per_batch * tm == s
    vec = lambda a: a.reshape(1, d)

    def attn_index(i):
        tile = jnp.maximum(i - FFN_CAST_STEPS, 0)
        return tile // per_batch, 0, tile % per_batch

    return pl.pallas_call(
        _outproj_ffn_kernel,
        grid=(FFN_CAST_STEPS + t // tm,),
        in_specs=([_row_tile_spec(tm, d), pl.BlockSpec((None, width, tm), attn_index),
                   _row_tile_spec(tm, omem.shape[1]),
                   _resident(w_out.shape), _resident((1, d)), _resident((1, d))]
                  + _ffn_weight_specs(layer, d) + [_resident((1, d)), _resident((1, d))]),
        out_specs=_row_tile_spec(tm, d),
        out_shape=jax.ShapeDtypeStruct((t, d), F32),
        scratch_shapes=_ffn_weight_scratch(d),
        compiler_params=_params("arbitrary"),
        name="outproj_ffn",
    )(x, attn_t, omem, w_out, vec(g_mix), vec(b_mix), w_in, w_ffn_out, vec(g), vec(b))


def _rotate_half_columns(w):
    half = w.shape[1] // 2
    return jnp.concatenate([-w[:, half:], w[:, :half]], axis=1)


def _head_tiles(cols_per_head, offset):
    k, h, c = cols_per_head.shape
    tiles = jnp.zeros((k, h, LANES), cols_per_head.dtype).at[:, :, offset:offset + c].set(cols_per_head)
    return tiles.reshape(k, h * LANES)


def _mla_weights(w_in, q_norm_g, w_uq, kv_norm_g, w_ukv):
    q_dim = MLA_NOPE_DIM + MLA_ROPE_DIM
    kr0 = MLA_Q_RANK + MLA_KV_RANK
    w_kr = w_in[:, kr0:kr0 + MLA_ROPE_DIM]
    uq = w_uq.reshape(MLA_Q_RANK, N_MLA_HEADS, q_dim)
    ukv = w_ukv.reshape(MLA_KV_RANK, N_MLA_HEADS, MLA_NOPE_DIM + MLA_V_DIM)
    as_tile = lambda w, off: _head_tiles(w[:, None, :], off)
    w = {
        "cq": w_in[:, :MLA_Q_RANK],
        "ckv": w_in[:, MLA_Q_RANK:kr0],
        "kr": as_tile(w_kr, MLA_NOPE_DIM),
        "kr_rot": as_tile(_rotate_half_columns(w_kr), MLA_NOPE_DIM),
        "qmem": w_in[:, kr0 + MLA_ROPE_DIM:],
        "q": _head_tiles(uq, 0),
        "k": _head_tiles(ukv[:, :, :MLA_NOPE_DIM], 0),
        "v": ukv[:, :, MLA_NOPE_DIM:].reshape(MLA_KV_RANK, MLA_WIDTH),
    }
    w = {name: a.astype(BF16) for name, a in w.items()}
    w["gq"] = q_norm_g.reshape(1, MLA_Q_RANK)
    w["gkv"] = kv_norm_g.reshape(1, MLA_KV_RANK)
    return w


def _rope_tables(s):
    freqs = ROPE_BASE ** (-jnp.arange(0, MLA_ROPE_DIM, 2, dtype=F32) / MLA_ROPE_DIM)
    ang = jnp.arange(s).astype(F32)[:, None] * freqs[None, :]
    pad = jnp.zeros((s, LANES - MLA_NOPE_DIM - MLA_ROPE_DIM), F32)
    cos = jnp.concatenate([jnp.ones((s, MLA_NOPE_DIM), F32), jnp.cos(ang), jnp.cos(ang), pad], axis=1)
    sin = jnp.concatenate([jnp.zeros((s, MLA_NOPE_DIM), F32), jnp.sin(ang), jnp.sin(ang), pad], axis=1)
    return cos, sin


def kernel(x, mem, ln_ffn1_g, ln_ffn1_b, ln_mix_g, ln_mix_b, ln_ffn2_g, ln_ffn2_b, ffn1_w_in, ffn1_w_out, ffn2_w_in, ffn2_w_out, sb_w_in, mla_w_in, mla_q_norm_g, mla_w_uq, mla_kv_norm_g, mla_w_ukv, mem_w_kv, w_out):
    b, s, d = x.shape
    flat = lambda a: a.reshape(b * s, a.shape[-1])
    cos, sin = _rope_tables(s)
    for i in range(DEPTH):
        x = _ffn_ln(flat(x), ffn1_w_in, ffn1_w_out, i, ln_ffn1_g[i], ln_ffn1_b[i]).reshape(b, s, d)
        km, vm = _mem_kv(mem, mem_w_kv[i].astype(BF16))
        j = i // 2
        if i % 2 == 0:
            qt, k, vt, omem = _sb_inproj(x, sb_w_in[j].astype(BF16), km, vm)
            attn = _sb_attention(qt, k, vt)
        else:
            w = _mla_weights(mla_w_in[j], mla_q_norm_g[j], mla_w_uq[j], mla_kv_norm_g[j], mla_w_ukv[j])
            qt, k, vt, omem = _mla_inproj(x, w, cos, sin, km, vm)
            attn = _mla_attention(qt, k, vt)
        x = _outproj_ffn(flat(x), attn, flat(omem), w_out[i].astype(BF16), ln_mix_g[i], ln_mix_b[i],
                         ffn2_w_in, ffn2_w_out, i, ln_ffn2_g[i], ln_ffn2_b[i]).reshape(b, s, d)
    return x
```
